```python
import jax, jax.numpy as jnp
from jax import lax
import numpy as np

D_MODEL = 1024
BATCH = 16
SEQ = 2048
DEPTH = 4

RET_HEADS = D_MODEL // 128
RET_QK_DIM = 64
RET_V_DIM = 128
RET_QK = RET_HEADS * RET_QK_DIM
RET_V = RET_HEADS * RET_V_DIM
CHUNK = 128
ROPE_BASE = 10000.0
POOL_WINDOWS = (2, 4, 8, 16)
POOL_GROUPS = len(POOL_WINDOWS)
POOL_DIM = D_MODEL // 2
POOL_GROUP_DIM = POOL_DIM // POOL_GROUPS
N_BRANCH = 2
IN_COLS = 2 * RET_QK + 2 * RET_V + POOL_DIM + N_BRANCH * D_MODEL
IN_SPLITS = (RET_QK, 2 * RET_QK, 2 * RET_QK + RET_V, 2 * RET_QK + 2 * RET_V,
             2 * RET_QK + 2 * RET_V + POOL_DIM)
_FF_RAW = -(-8 * D_MODEL // 3)
D_FF = -(-_FF_RAW // 256) * 256
N_MOD = 6
EPS = 1e-6

kernel_name = "hybrid_retention_pool_adaln_block"


def rmsnorm(x, w):
    xf = x.astype(jnp.float32)
    y = xf * lax.rsqrt(jnp.mean(xf * xf, axis=-1, keepdims=True) + EPS)
    return (y * w.astype(jnp.float32)).astype(x.dtype)


def head_rmsnorm(x):
    xf = x.astype(jnp.float32)
    y = xf * lax.rsqrt(jnp.mean(xf * xf, axis=-1, keepdims=True) + EPS)
    return y.astype(x.dtype)


def modulate(h, shift, scale):
    return h * (1.0 + scale[:, None, :]) + shift[:, None, :]


def rotary(x, cos, sin):
    x1, x2 = jnp.split(x, 2, axis=-1)
    return jnp.concatenate([x1 * cos - x2 * sin, x2 * cos + x1 * sin], axis=-1)


def retention_chunkwise(q, k, v):
    bsz, s, h, dk = q.shape
    dv = v.shape[-1]
    n = s // CHUNK
    dt = q.dtype
    log_g = jnp.log1p(-(2.0 ** (-5.0 - jnp.arange(h, dtype=jnp.float32))))
    idx = jnp.arange(CHUNK, dtype=jnp.float32)
    diff = idx[:, None] - idx[None, :]
    intra = jnp.where(diff >= 0, jnp.exp(log_g[:, None, None] * jnp.maximum(diff, 0.0)), 0.0).astype(dt)
    q_dec = jnp.exp(log_g[:, None] * (idx + 1.0)).astype(dt)
    k_dec = jnp.exp(log_g[:, None] * (CHUNK - 1.0 - idx)).astype(dt)
    chunk_dec = jnp.exp(log_g * CHUNK).astype(dt)
    qc = (q * (dk ** -0.5)).reshape(bsz, n, CHUNK, h, dk)
    kc = k.reshape(bsz, n, CHUNK, h, dk)
    vc = v.reshape(bsz, n, CHUNK, h, dv)
    scores = jnp.einsum('bncha,bnmha->bnhcm', qc, kc) * intra[None, None]
    inner = jnp.einsum('bnhcm,bnmhd->bnchd', scores, vc)
    kv = jnp.einsum('bnmha,hm,bnmhd->bnhad', kc, k_dec, vc)

    def step(state, kv_i):
        return state * chunk_dec[None, :, None, None] + kv_i, state

    init = jnp.zeros((bsz, h, dk, dv), dtype=kv.dtype)
    _, prev = lax.scan(step, init, jnp.moveaxis(kv, 1, 0))
    prev = jnp.moveaxis(prev, 0, 1)
    cross = jnp.einsum('bncha,hc,bnhad->bnchd', qc, q_dec, prev)
    return (inner + cross).reshape(bsz, s, h, dv)


def causal_multiscale_pool(p):
    bsz, s, _ = p.shape
    pg = p.reshape(bsz, s, POOL_GROUPS, POOL_GROUP_DIM)
    cs = jnp.cumsum(pg.astype(jnp.float32), axis=1)
    t = jnp.arange(s, dtype=jnp.float32)
    outs = []
    for g, w in enumerate(POOL_WINDOWS):
        csg = cs[:, :, g]
        lag = jnp.pad(csg[:, :s - w], ((0, 0), (w, 0), (0, 0)))
        cnt = jnp.minimum(t + 1.0, float(w))
        outs.append((csg - lag) / cnt[None, :, None])
    mean = jnp.stack(outs, axis=2).astype(p.dtype)
    return mean - pg


def setup_inputs(seed: int = 0) -> dict:
    key = jax.random.key(seed)
    ks = jax.random.split(key, 16)
    f32 = jnp.float32
    d = D_MODEL

    def w(k, shape, fan_in, mult=1.0):
        return jax.random.normal(k, shape, f32) * (mult * fan_in ** -0.5)

    return {
        "x": jax.random.normal(ks[0], (BATCH, SEQ, d), f32),
        "c": jax.random.normal(ks[1], (BATCH, d), f32),
        "w_ada": w(ks[2], (DEPTH, d, N_MOD * d), d, 0.5),
        "b_ada": 0.01 * jax.random.normal(ks[3], (DEPTH, N_MOD * d), f32),
        "norm1": 1.0 + 0.02 * jax.random.normal(ks[4], (DEPTH, d), f32),
        "w_in": w(ks[5], (DEPTH, d, IN_COLS), d),
        "w_ret_o": w(ks[6], (DEPTH, RET_V, d), RET_V),
        "w_pool_grp": w(ks[7], (DEPTH, POOL_GROUPS, POOL_GROUP_DIM, POOL_GROUP_DIM), POOL_GROUP_DIM),
        "pool_scale": 1.0 + 0.1 * jax.random.normal(ks[8], (DEPTH, POOL_DIM), f32),
        "w_pool_o": w(ks[9], (DEPTH, POOL_DIM, d), POOL_DIM),
        "w_out": w(ks[10], (DEPTH, d, d), d),
        "norm2": 1.0 + 0.02 * jax.random.normal(ks[11], (DEPTH, d), f32),
        "w_ffn_in": w(ks[12], (DEPTH, d, 2 * D_FF), d),
        "w_ffn_out": w(ks[13], (DEPTH, D_FF, d), D_FF),
        "final_norm": 1.0 + 0.02 * jax.random.normal(ks[14], (d,), f32),
    }


def reference(x, c, w_ada, b_ada, norm1, w_in, w_ret_o, w_pool_grp, pool_scale, w_pool_o,
              w_out, norm2, w_ffn_in, w_ffn_out, final_norm):
    bsz, s, _ = x.shape
    half = RET_QK_DIM // 2
    inv_freq = ROPE_BASE ** (-jnp.arange(half, dtype=jnp.float32) / half)
    ang = jnp.arange(s, dtype=jnp.float32)[:, None] * inv_freq[None, :]
    cos = jnp.cos(ang)[:, None, :].astype(x.dtype)
    sin = jnp.sin(ang)[:, None, :].astype(x.dtype)
    c_act = jax.nn.silu(c)

    for l in range(DEPTH):
        mod = c_act @ w_ada[l] + b_ada[l]
        sh1, sc1, g1, sh2, sc2, g2 = jnp.split(mod, N_MOD, axis=-1)

        h = modulate(rmsnorm(x, norm1[l]), sh1, sc1)
        proj = h @ w_in[l]
        q, k, v, g_sw, p_in, gates = jnp.split(proj, IN_SPLITS, axis=-1)
        q = rotary(q.reshape(bsz, s, RET_HEADS, RET_QK_DIM), cos, sin)
        k = rotary(k.reshape(bsz, s, RET_HEADS, RET_QK_DIM), cos, sin)
        v = v.reshape(bsz, s, RET_HEADS, RET_V_DIM)
        ret = head_rmsnorm(retention_chunkwise(q, k, v)).reshape(bsz, s, RET_V)
        ret_d = (jax.nn.silu(g_sw) * ret) @ w_ret_o[l]

        pooled = causal_multiscale_pool(p_in)
        pooled = jnp.einsum('bsgi,gio->bsgo', pooled, w_pool_grp[l]).reshape(bsz, s, POOL_DIM)
        pool_d = (pooled * pool_scale[l]) @ w_pool_o[l]

        a_ret, a_pool = jnp.split(gates, N_BRANCH, axis=-1)
        merged = jax.nn.sigmoid(a_ret) * ret_d + jax.nn.sigmoid(a_pool) * pool_d
        x = x + g1[:, None, :] * (merged @ w_out[l])

        h2 = modulate(rmsnorm(x, norm2[l]), sh2, sc2)
        gate, up = jnp.split(h2 @ w_ffn_in[l], 2, axis=-1)
        x = x + g2[:, None, :] * ((jax.nn.silu(gate) * up) @ w_ffn_out[l])

    return rmsnorm(x, final_norm)
```

```python
import functools

import jax
import jax.numpy as jnp
from jax import lax
from jax.experimental import pallas as pl
from jax.experimental.pallas import tpu as pltpu

D_MODEL = 1024
RET_HEADS = 8
RET_QK_DIM = 64
RET_V_DIM = 128
RET_QK = RET_HEADS * RET_QK_DIM
RET_V = RET_HEADS * RET_V_DIM
CHUNK = 128
ROPE_BASE = 10000.0
POOL_WINDOWS = (2, 4, 8, 16)
POOL_DIM = 512
POOL_GROUP_DIM = 128
POOL_HALO = 16
D_FF = 2816
N_MOD = 6
EPS = 1e-6

LANES = 128
MXU_N = 256
HALF = RET_QK_DIM // 2
HEAD_PAIRS = RET_HEADS // 2

MIX_ROWS = 256
FFN_ROWS = 256
VMEM_LIMIT = 56 * 1024 * 1024

F32 = jnp.float32
BF16 = jnp.bfloat16


def _dot(a, b):
    return jnp.dot(a, b, preferred_element_type=F32)


def _sigmoid(x):
    return 1.0 / (1.0 + jnp.exp(-x))


def _rms_scale(x):
    return x * lax.rsqrt(jnp.mean(x * x, axis=-1, keepdims=True) + EPS)


def _resident(shape, index_map):
    return pl.BlockSpec(shape, index_map, pipeline_mode=pl.Buffered(1))


def _ada_kernel(c_ref, w_ref, b_ref, o_ref):
    c = c_ref[...]
    ca = (c * _sigmoid(c)).astype(BF16)
    o_ref[0] = _dot(ca, w_ref[0].astype(BF16)) + b_ref[0]


def _ada_call(c, w_ada, b_ada):
    depth, d, n = w_ada.shape
    bsz = c.shape[0]
    bn = 1024
    return pl.pallas_call(
        _ada_kernel,
        grid=(depth, n // bn),
        in_specs=[
            pl.BlockSpec((bsz, d), lambda l, j: (0, 0)),
            pl.BlockSpec((1, d, bn), lambda l, j: (l, 0, j)),
            pl.BlockSpec((1, 1, bn), lambda l, j: (l, 0, j)),
        ],
        out_specs=pl.BlockSpec((1, bsz, bn), lambda l, j: (l, 0, j)),
        out_shape=jax.ShapeDtypeStruct((depth, bsz, n), F32),
        compiler_params=pltpu.CompilerParams(
            dimension_semantics=("arbitrary", "arbitrary"),
            vmem_limit_bytes=VMEM_LIMIT),
        name="adaln_mod",
    )(c, w_ada, b_ada.reshape(depth, 1, n))


def _mixer_kernel(x_ref, mod_ref, n1_ref, cos_ref, sin_ref,
                  wq_ref, wk_ref, wv_ref, wg_ref, wp_ref, wa_ref,
                  wro_ref, wpg_ref, ps_ref, wpo_ref, wout_ref,
                  dtab_ref, qd_ref, kd_ref, km_ref, cd_ref,
                  o_ref,
                  h_s, q_s, k_s, v_s, ret_s, ra_s, pbuf_s, pa_s, mg_s, state_s):
    rows = x_ref.shape[1]
    s_idx = pl.program_id(1)

    @pl.when(s_idx == 0)
    def _():
        state_s[...] = jnp.zeros_like(state_s)
        pbuf_s[0:POOL_HALO, :] = jnp.zeros((POOL_HALO, POOL_DIM), F32)

    shift = mod_ref[0, 0:1, :]
    scale = mod_ref[0, 1:2, :]
    gate_res = mod_ref[0, 2:3, :]

    x = x_ref[0]
    h = (_rms_scale(x) * n1_ref[0]) * (1.0 + scale) + shift
    h_s[...] = h.astype(BF16)

    cos = cos_ref[...]
    sin = sin_ref[...]
    q = _dot(h_s[...], wq_ref[0])
    k = _dot(h_s[...], wk_ref[0])
    for p in range(HEAD_PAIRS):
        sl = slice(p * LANES, (p + 1) * LANES)
        qp = q[:, sl]
        kp = k[:, sl]
        q_s[:, sl] = qp * cos + pltpu.roll(qp, 2 * HALF, axis=1) * sin
        k_s[:, sl] = kp * cos + pltpu.roll(kp, 2 * HALF, axis=1) * sin

    for n in range(RET_V // MXU_N):
        cs = slice(n * MXU_N, (n + 1) * MXU_N)
        v_s[:, cs] = _dot(h_s[...], wv_ref[0, :, cs]).astype(BF16)

    for p in range(HEAD_PAIRS):
        sl = slice(p * LANES, (p + 1) * LANES)
        for c in range(rows // CHUNK):
            rs = slice(c * CHUNK, (c + 1) * CHUNK)
            qc = q_s[rs, sl]
            kc = k_s[rs, sl]
            qb = qc.astype(BF16)
            for hh in range(2):
                head = 2 * p + hh
                hs = slice(head * RET_V_DIM, (head + 1) * RET_V_DIM)
                km = (kc * km_ref[head]).astype(BF16)
                scores = lax.dot_general(qb, km, (((1,), (1,)), ((), ())),
                                         preferred_element_type=F32)
                sb = (scores * dtab_ref[head]).astype(BF16)
                qd = (qc * qd_ref[head]).astype(BF16)
                vh = v_s[rs, hs]
                st = state_s[head]
                out = _dot(sb, vh) + _dot(qd, st.astype(BF16))
                kd = (kc * kd_ref[head]).astype(BF16)
                kv = lax.dot_general(kd, vh, (((0,), (0,)), ((), ())),
                                     preferred_element_type=F32)
                state_s[head] = st * cd_ref[head] + kv
                ret_s[rs, hs] = _rms_scale(out)

    for n in range(RET_V // MXU_N):
        cs = slice(n * MXU_N, (n + 1) * MXU_N)
        g = _dot(h_s[...], wg_ref[0, :, cs])
        ra_s[:, cs] = (g * _sigmoid(g) * ret_s[:, cs]).astype(BF16)

    pbuf_s[POOL_HALO:POOL_HALO + rows, :] = _dot(h_s[...], wp_ref[0])
    pos = s_idx * rows + lax.broadcasted_iota(jnp.int32, (rows, 1), 0)
    for g, w in enumerate(POOL_WINDOWS):
        cs = slice(g * POOL_GROUP_DIM, (g + 1) * POOL_GROUP_DIM)
        cur = pbuf_s[POOL_HALO:POOL_HALO + rows, cs]
        acc = cur
        for j in range(1, w):
            acc = acc + pbuf_s[POOL_HALO - j:POOL_HALO - j + rows, cs]
        cnt = jnp.minimum(pos + 1, w).astype(F32)
        pooled = (acc / cnt - cur).astype(BF16)
        y = _dot(pooled, wpg_ref[0, g]) * ps_ref[0, :, cs]
        pa_s[:, cs] = y.astype(BF16)
    pbuf_s[0:POOL_HALO, :] = pbuf_s[rows:rows + POOL_HALO, :]

    for n in range(D_MODEL // MXU_N):
        cs = slice(n * MXU_N, (n + 1) * MXU_N)
        cs2 = slice(D_MODEL + n * MXU_N, D_MODEL + (n + 1) * MXU_N)
        a_ret = _dot(h_s[...], wa_ref[0, :, cs])
        a_pool = _dot(h_s[...], wa_ref[0, :, cs2])
        ret_d = _dot(ra_s[...], wro_ref[0, :, cs])
        pool_d = _dot(pa_s[...], wpo_ref[0, :, cs])
        mg_s[:, cs] = (_sigmoid(a_ret) * ret_d + _sigmoid(a_pool) * pool_d).astype(BF16)

    for n in range(D_MODEL // MXU_N):
        cs = slice(n * MXU_N, (n + 1) * MXU_N)
        y = _dot(mg_s[...], wout_ref[0, :, cs])
        o_ref[0, :, cs] = x_ref[0, :, cs] + gate_res[:, cs] * y


def _mixer_call(layer, x, mod, norm1, cos, sin, wts, tabs):
    bsz, seq, d = x.shape
    rows = MIX_ROWS
    wq, wk, wv, wg, wp, wa, wro, wpg, ps, wpo, wout = wts
    dtab, qd, kd, km, cd = tabs

    def lw(arr):
        shape = (1,) + arr.shape[1:]
        zeros = (0,) * (arr.ndim - 1)
        return _resident(shape, lambda b, s: (layer,) + zeros)

    def const(arr):
        zeros = (0,) * arr.ndim
        return _resident(arr.shape, lambda b, s: zeros)

    in_specs = [
        pl.BlockSpec((1, rows, d), lambda b, s: (b, s, 0)),
        pl.BlockSpec((1, N_MOD, d), lambda b, s: (b, 0, 0)),
        lw(norm1),
        pl.BlockSpec((rows, LANES), lambda b, s: (s, 0)),
        pl.BlockSpec((rows, LANES), lambda b, s: (s, 0)),
        lw(wq), lw(wk), lw(wv), lw(wg), lw(wp), lw(wa),
        lw(wro), lw(wpg), lw(ps), lw(wpo), lw(wout),
        const(dtab), const(qd), const(kd), const(km), const(cd),
    ]
    scratch = [
        pltpu.VMEM((rows, d), BF16),
        pltpu.VMEM((rows, RET_QK), F32),
        pltpu.VMEM((rows, RET_QK), F32),
        pltpu.VMEM((rows, RET_V), BF16),
        pltpu.VMEM((rows, RET_V), F32),
        pltpu.VMEM((rows, RET_V), BF16),
        pltpu.VMEM((rows + POOL_HALO, POOL_DIM), F32),
        pltpu.VMEM((rows, POOL_DIM), BF16),
        pltpu.VMEM((rows, d), BF16),
        pltpu.VMEM((RET_HEADS, LANES, RET_V_DIM), F32),
    ]
    return pl.pallas_call(
        _mixer_kernel,
        grid=(bsz, seq // rows),
        in_specs=in_specs,
        out_specs=pl.BlockSpec((1, rows, d), lambda b, s: (b, s, 0)),
        out_shape=jax.ShapeDtypeStruct(x.shape, F32),
        scratch_shapes=scratch,
        compiler_params=pltpu.CompilerParams(
            dimension_semantics=("arbitrary", "arbitrary"),
            vmem_limit_bytes=VMEM_LIMIT),
        name=f"mixer_l{layer}",
    )(x, mod, norm1, cos, sin, wq, wk, wv, wg, wp, wa, wro, wpg, ps, wpo, wout,
      dtab, qd, kd, km, cd)


def _ffn_kernel(x_ref, mod_ref, n2_ref, wi_ref, wo_ref, fn_ref, o_ref, h_s, act_s, *, final):
    shift = mod_ref[0, 3:4, :]
    scale = mod_ref[0, 4:5, :]
    gate_res = mod_ref[0, 5:6, :]

    x = x_ref[...]
    h = (_rms_scale(x) * n2_ref[0]) * (1.0 + scale) + shift
    h_s[...] = h.astype(BF16)

    for j in range(D_FF // MXU_N):
        cs = slice(j * MXU_N, (j + 1) * MXU_N)
        cs2 = slice(D_FF + j * MXU_N, D_FF + (j + 1) * MXU_N)
        gate = _dot(h_s[...], wi_ref[0, :, cs])
        up = _dot(h_s[...], wi_ref[0, :, cs2])
        act_s[:, cs] = (gate * _sigmoid(gate) * up).astype(BF16)

    for n in range(D_MODEL // MXU_N):
        cs = slice(n * MXU_N, (n + 1) * MXU_N)
        y = _dot(act_s[...], wo_ref[0, :, cs])
        o_ref[:, cs] = x_ref[:, cs] + gate_res[:, cs] * y

    if final:
        o_ref[...] = _rms_scale(o_ref[...]) * fn_ref[...]


def _ffn_call(layer, x2d, mod, norm2, wi, wo, final_norm, seq, final):
    n_rows, d = x2d.shape
    rows = FFN_ROWS
    tiles_per_seq = seq // rows
    return pl.pallas_call(
        functools.partial(_ffn_kernel, final=final),
        grid=(n_rows // rows,),
        in_specs=[
            pl.BlockSpec((rows, d), lambda i: (i, 0)),
            pl.BlockSpec((1, N_MOD, d), lambda i: (i // tiles_per_seq, 0, 0)),
            _resident((1, 1, d), lambda i: (layer, 0, 0)),
            _resident((1, d, 2 * D_FF), lambda i: (layer, 0, 0)),
            _resident((1, D_FF, d), lambda i: (layer, 0, 0)),
            _resident((1, d), lambda i: (0, 0)),
        ],
        out_specs=pl.BlockSpec((rows, d), lambda i: (i, 0)),
        out_shape=jax.ShapeDtypeStruct(x2d.shape, F32),
        scratch_shapes=[
            pltpu.VMEM((rows, d), BF16),
            pltpu.VMEM((rows, D_FF), BF16),
        ],
        compiler_params=pltpu.CompilerParams(
            dimension_semantics=("arbitrary",),
            vmem_limit_bytes=VMEM_LIMIT),
        name=f"swiglu_l{layer}",
    )(x2d, mod, norm2, wi, wo, final_norm)


def _qk_column_order():
    order = []
    for p in range(HEAD_PAIRS):
        for lane in range(LANES):
            blk, r = divmod(lane, HALF)
            head = 2 * p + (blk % 2)
            j = r + HALF * (blk // 2)
            order.append(head * RET_QK_DIM + j)
    return jnp.asarray(order, dtype=jnp.int32)


def _retention_tables():
    heads = jnp.arange(RET_HEADS, dtype=F32)
    log_g = jnp.log1p(-(2.0 ** (-5.0 - heads)))
    idx = jnp.arange(CHUNK, dtype=F32)
    diff = idx[:, None] - idx[None, :]
    qscale = RET_QK_DIM ** -0.5
    intra = jnp.where(diff >= 0, jnp.exp(log_g[:, None, None] * jnp.maximum(diff, 0.0)), 0.0)
    dtab = intra * qscale
    q_dec = jnp.exp(log_g[:, None] * (idx + 1.0)) * qscale
    k_dec = jnp.exp(log_g[:, None] * (CHUNK - 1.0 - idx))
    chunk_dec = jnp.exp(log_g * CHUNK)
    lane = jnp.arange(LANES)
    parity = (lane // HALF) % 2
    mask = (parity[None, :] == (jnp.arange(RET_HEADS) % 2)[:, None]).astype(F32)
    qd = jnp.broadcast_to(q_dec[:, :, None], (RET_HEADS, CHUNK, LANES))
    kd = k_dec[:, :, None] * mask[:, None, :]
    km = mask[:, None, :]
    cd = jnp.broadcast_to(chunk_dec[:, None, None], (RET_HEADS, 1, RET_V_DIM))
    return dtab, qd, kd, km, cd


def _rotary_tables(seq):
    inv_freq = ROPE_BASE ** (-jnp.arange(HALF, dtype=F32) / HALF)
    ang = jnp.arange(seq, dtype=F32)[:, None] * inv_freq[None, :]
    cos = jnp.cos(ang)
    sin = jnp.sin(ang)
    cos4 = jnp.concatenate([cos, cos, cos, cos], axis=-1)
    sin4 = jnp.concatenate([-sin, -sin, sin, sin], axis=-1)
    return cos4, sin4


def kernel(x, c, w_ada, b_ada, norm1, w_in, w_ret_o, w_pool_grp, pool_scale, w_pool_o,
           w_out, norm2, w_ffn_in, w_ffn_out, final_norm):
    bsz, seq, d = x.shape
    depth = w_in.shape[0]
    assert seq % MIX_ROWS == 0 and seq % FFN_ROWS == 0 and MIX_ROWS % CHUNK == 0

    mod = _ada_call(c, w_ada, b_ada).reshape(depth, bsz, N_MOD, d)

    order = _qk_column_order()
    o0, o1, o2, o3, o4 = RET_QK, 2 * RET_QK, 2 * RET_QK + RET_V, 2 * RET_QK + 2 * RET_V, \
        2 * RET_QK + 2 * RET_V + POOL_DIM
    wq = jnp.take(w_in[:, :, :o0], order, axis=2).astype(BF16)
    wk = jnp.take(w_in[:, :, o0:o1], order, axis=2).astype(BF16)
    wv = w_in[:, :, o1:o2].astype(BF16)
    wg = w_in[:, :, o2:o3].astype(BF16)
    wp = w_in[:, :, o3:o4].astype(BF16)
    wa = w_in[:, :, o4:].astype(BF16)
    wts = (wq, wk, wv, wg, wp, wa, w_ret_o.astype(BF16), w_pool_grp.astype(BF16),
           pool_scale.reshape(depth, 1, POOL_DIM), w_pool_o.astype(BF16), w_out.astype(BF16))
    wi = w_ffn_in.astype(BF16)
    wo = w_ffn_out.astype(BF16)
    n1 = norm1.reshape(depth, 1, d)
    n2 = norm2.reshape(depth, 1, d)
    fn = final_norm.reshape(1, d)

    tabs = _retention_tables()
    cos4, sin4 = _rotary_tables(seq)

    for l in range(depth):
        x = _mixer_call(l, x, mod[l], n1, cos4, sin4, wts, tabs)
        x2d = _ffn_call(l, x.reshape(bsz * seq, d), mod[l], n2, wi, wo, fn, seq,
                        final=(l == depth - 1))
        x = x2d.reshape(bsz, seq, d)
    return x
```

```python
import functools

import jax
import jax.numpy as jnp
from jax import lax
from jax.experimental import pallas as pl
from jax.experimental.pallas import tpu as pltpu

D_MODEL = 1024
RET_HEADS = 8
RET_QK_DIM = 64
RET_V_DIM = 128
RET_QK = RET_HEADS * RET_QK_DIM
RET_V = RET_HEADS * RET_V_DIM
CHUNK = 128
ROPE_BASE = 10000.0
POOL_WINDOWS = (2, 4, 8, 16)
POOL_DIM = 512
POOL_GROUP_DIM = 128
POOL_HALO = 16
D_FF = 2816
N_MOD = 6
EPS = 1e-6

LANES = 128
SUBLANES = 8
MXU_N = 256
HALF = RET_QK_DIM // 2
HEAD_PAIRS = RET_HEADS // 2

MIX_ROWS = 512
FFN_ROWS = 512
VMEM_LIMIT = 56 * 1024 * 1024

F32 = jnp.float32
BF16 = jnp.bfloat16


def _dot(a, b):
    return jnp.dot(a, b, preferred_element_type=F32)


def _sigmoid(x):
    return 1.0 / (1.0 + jnp.exp(-x))


def _rms_scale(x):
    return x * lax.rsqrt(jnp.mean(x * x, axis=-1, keepdims=True) + EPS)


def _zero_token(v):
    bits = pltpu.bitcast(v, jnp.int32)
    n_rows, n_cols = bits.shape
    acc = bits[0:SUBLANES]
    for i in range(1, n_rows // SUBLANES):
        acc = acc | bits[i * SUBLANES:(i + 1) * SUBLANES]
    tok = acc[:, 0:LANES]
    for j in range(1, n_cols // LANES):
        tok = tok | acc[:, j * LANES:(j + 1) * LANES]
    tok = lax.shift_right_logical(lax.shift_right_logical(tok, 16), 16)
    return tok.astype(F32)


def _with_token(val, tok):
    head = val[0:SUBLANES] + jnp.concatenate([tok] * (val.shape[1] // LANES), axis=1)
    return jnp.concatenate([head, val[SUBLANES:]], axis=0)


def _resident(shape, index_map):
    return pl.BlockSpec(shape, index_map, pipeline_mode=pl.Buffered(1))


def _ada_kernel(c_ref, w_ref, b_ref, o_ref):
    c = c_ref[...]
    ca = (c * _sigmoid(c)).astype(BF16)
    o_ref[0] = _dot(ca, w_ref[0].astype(BF16)) + b_ref[0]


def _ada_call(c, w_ada, b_ada):
    depth, d, n = w_ada.shape
    bsz = c.shape[0]
    bn = 1024
    return pl.pallas_call(
        _ada_kernel,
        grid=(depth, n // bn),
        in_specs=[
            pl.BlockSpec((bsz, d), lambda l, j: (0, 0)),
            pl.BlockSpec((1, d, bn), lambda l, j: (l, 0, j)),
            pl.BlockSpec((1, 1, bn), lambda l, j: (l, 0, j)),
        ],
        out_specs=pl.BlockSpec((1, bsz, bn), lambda l, j: (l, 0, j)),
        out_shape=jax.ShapeDtypeStruct((depth, bsz, n), F32),
        compiler_params=pltpu.CompilerParams(
            dimension_semantics=("arbitrary", "arbitrary"),
            vmem_limit_bytes=VMEM_LIMIT),
        name="adaln_mod",
    )(c, w_ada, b_ada.reshape(depth, 1, n))


def _mixer_kernel(x_ref, xn_ref, mod_ref, n1_ref, cos_ref, sin_ref,
                  wq_ref, wk_ref, wv_ref, wg_ref, wp_ref, wa_ref,
                  wro_ref, wpg_ref, ps_ref, wpo_ref, wout_ref,
                  dtab_ref, qd_ref, kd_ref, cd_ref,
                  o_ref,
                  h_s, hn_s, q_s, k_s, v_s, ret_s, ra_s, pbuf_s, pa_s, mg_s, state_s):
    rows = x_ref.shape[1]
    s_idx = pl.program_id(1)
    b_idx = pl.program_id(0)
    gate_res = mod_ref[b_idx, 2:3, :]

    def norm_mod(xv):
        return ((_rms_scale(xv) * n1_ref[0]) * (1.0 + mod_ref[b_idx, 1:2, :])
                + mod_ref[b_idx, 0:1, :]).astype(BF16)

    @pl.when(s_idx == 0)
    def _():
        state_s[...] = jnp.zeros_like(state_s)
        pbuf_s[0:POOL_HALO, :] = jnp.zeros((POOL_HALO, POOL_DIM), F32)
        h_s[...] = norm_mod(x_ref[0])

    @pl.when(s_idx != 0)
    def _():
        h_s[...] = hn_s[...]

    row0 = pl.multiple_of(s_idx * rows, rows)
    cos = cos_ref[pl.ds(row0, rows), :]
    sin = sin_ref[pl.ds(row0, rows), :]
    q = _dot(h_s[...], wq_ref[0])
    k = _dot(h_s[...], wk_ref[0])
    for p in range(HEAD_PAIRS):
        sl = slice(p * LANES, (p + 1) * LANES)
        qp = q[:, sl]
        kp = k[:, sl]
        q_s[:, sl] = qp * cos + pltpu.roll(qp, 2 * HALF, axis=1) * sin
        k_s[:, sl] = kp * cos + pltpu.roll(kp, 2 * HALF, axis=1) * sin

    n_v = RET_V // MXU_N
    part = rows // n_v
    for n in range(n_v):
        cs = slice(n * MXU_N, (n + 1) * MXU_N)
        rs = slice(n * part, (n + 1) * part)
        hb = norm_mod(xn_ref[0, rs, :])
        hn_s[rs, :] = hb
        v = _with_token(_dot(h_s[...], wv_ref[0, :, cs]), _zero_token(hb))
        v_s[:, cs] = v.astype(BF16)

    lane = lax.broadcasted_iota(jnp.int32, (CHUNK, LANES), 1)
    even_lane = ((lane >> 5) & 1) == 0
    row_odd = (lax.broadcasted_iota(jnp.int32, (LANES, 2 * RET_V_DIM), 0) >> 5) & 1
    col_odd = (lax.broadcasted_iota(jnp.int32, (LANES, 2 * RET_V_DIM), 1) >= RET_V_DIM).astype(jnp.int32)
    same_head = row_odd == col_odd
    zero_v = jnp.zeros((CHUNK, RET_V_DIM), BF16)
    for p in range(HEAD_PAIRS):
        sl = slice(p * LANES, (p + 1) * LANES)
        vs = slice(p * 2 * RET_V_DIM, (p + 1) * 2 * RET_V_DIM)
        st = state_s[p]
        for c in range(rows // CHUNK):
            rs = slice(c * CHUNK, (c + 1) * CHUNK)
            qc = q_s[rs, sl]
            kc = k_s[rs, sl]
            v_pair = v_s[rs, vs]
            k_cat = jnp.concatenate([jnp.where(even_lane, kc, 0.0).astype(BF16),
                                     jnp.where(even_lane, 0.0, kc).astype(BF16)], axis=0)
            scores = lax.dot_general(qc.astype(BF16), k_cat, (((1,), (1,)), ((), ())),
                                     preferred_element_type=F32)
            sb = (scores * dtab_ref[p]).astype(BF16)
            v_diag = jnp.concatenate(
                [jnp.concatenate([v_pair[:, :RET_V_DIM], zero_v], axis=1),
                 jnp.concatenate([zero_v, v_pair[:, RET_V_DIM:]], axis=1)], axis=0)
            qd = (qc * qd_ref[p]).astype(BF16)
            out = _dot(sb, v_diag) + _dot(qd, st.astype(BF16))
            kd = (kc * kd_ref[p]).astype(BF16)
            kv = lax.dot_general(kd, v_pair, (((0,), (0,)), ((), ())),
                                 preferred_element_type=F32)
            st = st * cd_ref[p] + jnp.where(same_head, kv, 0.0)
            for hh in range(2):
                hs = slice((2 * p + hh) * RET_V_DIM, (2 * p + hh + 1) * RET_V_DIM)
                ret_s[rs, hs] = _rms_scale(out[:, hh * RET_V_DIM:(hh + 1) * RET_V_DIM])
        state_s[p] = st

    for n in range(RET_V // MXU_N):
        cs = slice(n * MXU_N, (n + 1) * MXU_N)
        g = _dot(h_s[...], wg_ref[0, :, cs])
        ra_s[:, cs] = (g * _sigmoid(g) * ret_s[:, cs]).astype(BF16)

    pbuf_s[POOL_HALO:POOL_HALO + rows, :] = _dot(h_s[...], wp_ref[0])
    head_pos = s_idx * rows + lax.broadcasted_iota(jnp.int32, (POOL_HALO, POOL_GROUP_DIM), 0)
    pooled = []
    for g, w in enumerate(POOL_WINDOWS):
        cs = slice(g * POOL_GROUP_DIM, (g + 1) * POOL_GROUP_DIM)
        ext = pbuf_s[:, cs]
        cur = ext[POOL_HALO:]
        span = 1
        while span < w:
            ext = ext + pltpu.roll(ext, span, axis=0)
            span *= 2
        acc = ext[POOL_HALO:]
        head_inv = 1.0 / jnp.minimum(head_pos + 1, w).astype(F32)
        mean = jnp.concatenate([acc[:POOL_HALO] * head_inv, acc[POOL_HALO:] * (1.0 / w)], axis=0)
        pooled.append((mean - cur).astype(BF16))
    pbuf_s[0:POOL_HALO, :] = pbuf_s[rows:rows + POOL_HALO, :]
    for gp in range(len(POOL_WINDOWS) // 2):
        cs = slice(gp * MXU_N, (gp + 1) * MXU_N)
        pair = jnp.concatenate([pooled[2 * gp], pooled[2 * gp + 1]], axis=1)
        y = _dot(pair, wpg_ref[0, gp]) * ps_ref[0, :, cs]
        pa_s[:, cs] = y.astype(BF16)

    for n in range(D_MODEL // MXU_N):
        cs = slice(n * MXU_N, (n + 1) * MXU_N)
        cs2 = slice(D_MODEL + n * MXU_N, D_MODEL + (n + 1) * MXU_N)
        a_ret = _dot(h_s[...], wa_ref[0, :, cs])
        a_pool = _dot(h_s[...], wa_ref[0, :, cs2])
        ret_d = _dot(ra_s[...], wro_ref[0, :, cs])
        pool_d = _dot(pa_s[...], wpo_ref[0, :, cs])
        mg_s[:, cs] = (_sigmoid(a_ret) * ret_d + _sigmoid(a_pool) * pool_d).astype(BF16)

    for n in range(D_MODEL // MXU_N):
        cs = slice(n * MXU_N, (n + 1) * MXU_N)
        y = _dot(mg_s[...], wout_ref[0, :, cs])
        o_ref[0, :, cs] = x_ref[0, :, cs] + gate_res[:, cs] * y


def _mixer_call(layer, x, mod, norm1, cos, sin, wts, tabs):
    bsz, seq, d = x.shape
    rows = MIX_ROWS
    wq, wk, wv, wg, wp, wa, wro, wpg, ps, wpo, wout = wts
    dtab, qd, kd, cd = tabs

    def lw(arr):
        shape = (1,) + arr.shape[1:]
        zeros = (0,) * (arr.ndim - 1)
        return _resident(shape, lambda b, s: (layer,) + zeros)

    def const(arr):
        zeros = (0,) * arr.ndim
        return _resident(arr.shape, lambda b, s: zeros)

    last = seq // rows - 1
    in_specs = [
        pl.BlockSpec((1, rows, d), lambda b, s: (b, s, 0)),
        pl.BlockSpec((1, rows, d), lambda b, s: (b, jnp.minimum(s + 1, last), 0)),
        const(mod),
        lw(norm1),
        const(cos), const(sin),
        lw(wq), lw(wk), lw(wv), lw(wg), lw(wp), lw(wa),
        lw(wro), lw(wpg), lw(ps), lw(wpo), lw(wout),
        const(dtab), const(qd), const(kd), const(cd),
    ]
    scratch = [
        pltpu.VMEM((rows, d), BF16),
        pltpu.VMEM((rows, d), BF16),
        pltpu.VMEM((rows, RET_QK), F32),
        pltpu.VMEM((rows, RET_QK), F32),
        pltpu.VMEM((rows, RET_V), BF16),
        pltpu.VMEM((rows, RET_V), F32),
        pltpu.VMEM((rows, RET_V), BF16),
        pltpu.VMEM((rows + POOL_HALO, POOL_DIM), F32),
        pltpu.VMEM((rows, POOL_DIM), BF16),
        pltpu.VMEM((rows, d), BF16),
        pltpu.VMEM((HEAD_PAIRS, LANES, 2 * RET_V_DIM), F32),
    ]
    return pl.pallas_call(
        _mixer_kernel,
        grid=(bsz, seq // rows),
        in_specs=in_specs,
        out_specs=pl.BlockSpec((1, rows, d), lambda b, s: (b, s, 0)),
        out_shape=jax.ShapeDtypeStruct(x.shape, F32),
        scratch_shapes=scratch,
        compiler_params=pltpu.CompilerParams(
            dimension_semantics=("arbitrary", "arbitrary"),
            vmem_limit_bytes=VMEM_LIMIT),
        name=f"mixer_l{layer}",
    )(x, x, mod, norm1, cos, sin, wq, wk, wv, wg, wp, wa, wro, wpg, ps, wpo, wout,
      dtab, qd, kd, cd)


def _ffn_kernel(x_ref, xn_ref, mod_ref, n2_ref, wi_ref, wo_ref, fn_ref, o_ref,
                h_s, hn_s, act_s, *, final, tiles_per_seq):
    n_tiles = pl.num_programs(0)
    b_idx = pl.program_id(0) // tiles_per_seq
    bn_idx = jnp.minimum(pl.program_id(0) + 1, n_tiles - 1) // tiles_per_seq
    gate_res = mod_ref[b_idx, 5:6, :]

    def norm_mod(xv, b):
        return ((_rms_scale(xv) * n2_ref[0]) * (1.0 + mod_ref[b, 4:5, :]) + mod_ref[b, 3:4, :]).astype(BF16)

    @pl.when(pl.program_id(0) == 0)
    def _():
        h_s[...] = norm_mod(x_ref[...], b_idx)

    @pl.when(pl.program_id(0) != 0)
    def _():
        h_s[...] = hn_s[...]

    n_chunks = D_FF // MXU_N
    n_parts = 4
    part = x_ref.shape[0] // n_parts
    for j in range(n_chunks):
        cs = slice(j * MXU_N, (j + 1) * MXU_N)
        cs2 = slice(D_FF + j * MXU_N, D_FF + (j + 1) * MXU_N)
        gate = _dot(h_s[...], wi_ref[0, :, cs])
        up = _dot(h_s[...], wi_ref[0, :, cs2])
        act = gate * _sigmoid(gate) * up
        if j % 3 == 1 and j // 3 < n_parts:
            rs = slice((j // 3) * part, (j // 3 + 1) * part)
            hb = norm_mod(xn_ref[rs, :], bn_idx)
            hn_s[rs, :] = hb
            act = _with_token(act, _zero_token(hb))
        act_s[:, cs] = act.astype(BF16)

    for n in range(D_MODEL // MXU_N):
        cs = slice(n * MXU_N, (n + 1) * MXU_N)
        y = _dot(act_s[...], wo_ref[0, :, cs])
        o_ref[:, cs] = x_ref[:, cs] + gate_res[:, cs] * y

    if final:
        o_ref[...] = _rms_scale(o_ref[...]) * fn_ref[...]


def _ffn_call(layer, x2d, mod, norm2, wi, wo, final_norm, seq, final):
    n_rows, d = x2d.shape
    rows = FFN_ROWS
    tiles_per_seq = seq // rows
    last = n_rows // rows - 1

    return pl.pallas_call(
        functools.partial(_ffn_kernel, final=final, tiles_per_seq=tiles_per_seq),
        grid=(n_rows // rows,),
        in_specs=[
            pl.BlockSpec((rows, d), lambda i: (i, 0)),
            pl.BlockSpec((rows, d), lambda i: (jnp.minimum(i + 1, last), 0)),
            _resident(mod.shape, lambda i: (0, 0, 0)),
            _resident((1, 1, d), lambda i: (layer, 0, 0)),
            _resident((1, d, 2 * D_FF), lambda i: (layer, 0, 0)),
            _resident((1, D_FF, d), lambda i: (layer, 0, 0)),
            _resident((1, d), lambda i: (0, 0)),
        ],
        out_specs=pl.BlockSpec((rows, d), lambda i: (i, 0)),
        out_shape=jax.ShapeDtypeStruct(x2d.shape, F32),
        scratch_shapes=[
            pltpu.VMEM((rows, d), BF16),
            pltpu.VMEM((rows, d), BF16),
            pltpu.VMEM((rows, D_FF), BF16),
        ],
        compiler_params=pltpu.CompilerParams(
            dimension_semantics=("arbitrary",),
            vmem_limit_bytes=VMEM_LIMIT),
        name=f"swiglu_l{layer}",
    )(x2d, x2d, mod, norm2, wi, wo, final_norm)


def _qk_column_order():
    order = []
    for p in range(HEAD_PAIRS):
        for lane in range(LANES):
            blk, r = divmod(lane, HALF)
            head = 2 * p + (blk % 2)
            j = r + HALF * (blk // 2)
            order.append(head * RET_QK_DIM + j)
    return jnp.asarray(order, dtype=jnp.int32)


def _retention_tables():
    heads = jnp.arange(RET_HEADS, dtype=F32)
    log_g = jnp.log1p(-(2.0 ** (-5.0 - heads)))
    idx = jnp.arange(CHUNK, dtype=F32)
    diff = idx[:, None] - idx[None, :]
    qscale = RET_QK_DIM ** -0.5
    intra = jnp.where(diff >= 0, jnp.exp(log_g[:, None, None] * jnp.maximum(diff, 0.0)), 0.0)
    dtab = intra * qscale
    q_dec = jnp.exp(log_g[:, None] * (idx + 1.0)) * qscale
    k_dec = jnp.exp(log_g[:, None] * (CHUNK - 1.0 - idx))
    chunk_dec = jnp.exp(log_g * CHUNK)
    parity = (jnp.arange(LANES) // HALF) % 2
    pair_head = 2 * jnp.arange(HEAD_PAIRS)[:, None] + parity[None, :]
    dtab_p = jnp.concatenate([dtab[0::2], dtab[1::2]], axis=2)
    qd_p = jnp.transpose(q_dec[pair_head], (0, 2, 1))
    kd_p = jnp.transpose(k_dec[pair_head], (0, 2, 1))
    cd_p = jnp.concatenate(
        [jnp.broadcast_to(chunk_dec[0::2, None, None], (HEAD_PAIRS, 1, RET_V_DIM)),
         jnp.broadcast_to(chunk_dec[1::2, None, None], (HEAD_PAIRS, 1, RET_V_DIM))], axis=2)
    return dtab_p, qd_p, kd_p, cd_p


def _rotary_tables(seq):
    inv_freq = ROPE_BASE ** (-jnp.arange(HALF, dtype=F32) / HALF)
    ang = jnp.arange(seq, dtype=F32)[:, None] * inv_freq[None, :]
    cos = jnp.cos(ang)
    sin = jnp.sin(ang)
    cos4 = jnp.concatenate([cos, cos, cos, cos], axis=-1)
    sin4 = jnp.concatenate([-sin, -sin, sin, sin], axis=-1)
    return cos4, sin4


def kernel(x, c, w_ada, b_ada, norm1, w_in, w_ret_o, w_pool_grp, pool_scale, w_pool_o,
           w_out, norm2, w_ffn_in, w_ffn_out, final_norm):
    bsz, seq, d = x.shape
    depth = w_in.shape[0]
    assert seq % MIX_ROWS == 0 and seq % FFN_ROWS == 0 and MIX_ROWS % CHUNK == 0

    mod = _ada_call(c, w_ada, b_ada).reshape(depth, bsz, N_MOD, d)

    order = _qk_column_order()
    o0, o1, o2, o3, o4 = RET_QK, 2 * RET_QK, 2 * RET_QK + RET_V, 2 * RET_QK + 2 * RET_V, \
        2 * RET_QK + 2 * RET_V + POOL_DIM
    wq = jnp.take(w_in[:, :, :o0], order, axis=2).astype(BF16)
    wk = jnp.take(w_in[:, :, o0:o1], order, axis=2).astype(BF16)
    wv = w_in[:, :, o1:o2].astype(BF16)
    wg = w_in[:, :, o2:o3].astype(BF16)
    wp = w_in[:, :, o3:o4].astype(BF16)
    wa = w_in[:, :, o4:].astype(BF16)
    n_gp = len(POOL_WINDOWS) // 2
    wpg = w_pool_grp.astype(BF16).reshape(depth, n_gp, 2, POOL_GROUP_DIM, POOL_GROUP_DIM)
    wpg = jnp.einsum('lpkio,kq->lpkiqo', wpg, jnp.eye(2, dtype=BF16))
    wpg = wpg.reshape(depth, n_gp, 2 * POOL_GROUP_DIM, 2 * POOL_GROUP_DIM)
    wts = (wq, wk, wv, wg, wp, wa, w_ret_o.astype(BF16), wpg,
           pool_scale.reshape(depth, 1, POOL_DIM), w_pool_o.astype(BF16), w_out.astype(BF16))
    wi = w_ffn_in.astype(BF16)
    wo = w_ffn_out.astype(BF16)
    n1 = norm1.reshape(depth, 1, d)
    n2 = norm2.reshape(depth, 1, d)
    fn = final_norm.reshape(1, d)

    tabs = _retention_tables()
    cos4, sin4 = _rotary_tables(seq)

    for l in range(depth):
        x = _mixer_call(l, x, mod[l], n1, cos4, sin4, wts, tabs)
        x2d = _ffn_call(l, x.reshape(bsz * seq, d), mod[l], n2, wi, wo, fn, seq,
                        final=(l == depth - 1))
        x = x2d.reshape(bsz, seq, d)
    return x
```

```python
import functools

import jax
import jax.numpy as jnp
from jax import lax
from jax.experimental import pallas as pl
from jax.experimental.pallas import tpu as pltpu

D_MODEL = 1024
RET_HEADS = 8
RET_QK_DIM = 64
RET_V_DIM = 128
RET_QK = RET_HEADS * RET_QK_DIM
RET_V = RET_HEADS * RET_V_DIM
CHUNK = 128
ROPE_BASE = 10000.0
POOL_WINDOWS = (2, 4, 8, 16)
POOL_DIM = 512
POOL_GROUP_DIM = 128
POOL_HALO = 16
D_FF = 2816
N_MOD = 6
EPS = 1e-6

LANES = 128
SUBLANES = 8
MXU_N = 256
HALF = RET_QK_DIM // 2
HEAD_PAIRS = RET_HEADS // 2
O_QK = 0
O_V = 2 * RET_QK
O_GSW = O_V + RET_V
O_POOL = O_GSW + RET_V
O_GATES = O_POOL + POOL_DIM
IN_COLS = O_GATES + 2 * D_MODEL

MIX_ROWS = 512
FFN_ROWS = 512
VMEM_LIMIT = 56 * 1024 * 1024

F32 = jnp.float32
BF16 = jnp.bfloat16


def _dot(a, b):
    return jnp.dot(a, b, preferred_element_type=F32)


def _sigmoid(x):
    return 1.0 / (1.0 + jnp.exp(-x))


def _rms_scale(x):
    return x * lax.rsqrt(jnp.mean(x * x, axis=-1, keepdims=True) + EPS)


def _zero_token(v):
    bits = pltpu.bitcast(v, jnp.int32)
    n_rows, n_cols = bits.shape
    acc = bits[0:SUBLANES]
    for i in range(1, n_rows // SUBLANES):
        acc = acc | bits[i * SUBLANES:(i + 1) * SUBLANES]
    tok = acc[:, 0:LANES]
    for j in range(1, n_cols // LANES):
        tok = tok | acc[:, j * LANES:(j + 1) * LANES]
    tok = lax.shift_right_logical(lax.shift_right_logical(tok, 16), 16)
    return tok.astype(F32)


def _with_token(val, tok):
    head = val[0:SUBLANES] + jnp.concatenate([tok] * (val.shape[1] // LANES), axis=1)
    return jnp.concatenate([head, val[SUBLANES:]], axis=0)


def _resident(shape, index_map):
    return pl.BlockSpec(shape, index_map, pipeline_mode=pl.Buffered(1))


def _ada_kernel(c_ref, w_ref, b_ref, o_ref):
    c = c_ref[...]
    ca = (c * _sigmoid(c)).astype(BF16)
    o_ref[0] = _dot(ca, w_ref[0].astype(BF16)) + b_ref[0]


def _ada_call(c, w_ada, b_ada):
    depth, d, n = w_ada.shape
    bsz = c.shape[0]
    bn = 1024
    return pl.pallas_call(
        _ada_kernel,
        grid=(depth, n // bn),
        in_specs=[
            pl.BlockSpec((bsz, d), lambda l, j: (0, 0)),
            pl.BlockSpec((1, d, bn), lambda l, j: (l, 0, j)),
            pl.BlockSpec((1, 1, bn), lambda l, j: (l, 0, j)),
        ],
        out_specs=pl.BlockSpec((1, bsz, bn), lambda l, j: (l, 0, j)),
        out_shape=jax.ShapeDtypeStruct((depth, bsz, n), F32),
        compiler_params=pltpu.CompilerParams(
            dimension_semantics=("arbitrary", "arbitrary"),
            vmem_limit_bytes=VMEM_LIMIT),
        name="adaln_mod",
    )(c, w_ada, b_ada.reshape(depth, 1, n))


def _mixer_kernel(x_ref, xn_ref, mod_ref, n1_ref, cos_ref, sin_ref,
                  wi_ref, wro_ref, wpg_ref, ps_ref, wpo_ref, wout_ref,
                  dtab_ref, qd_ref, kd_ref, cd_ref,
                  o_ref,
                  h_s, hn_s, q_s, k_s, v_s, ret_s, ra_s, pbuf_s, pa_s, mg_s, state_s):
    rows = x_ref.shape[1]
    s_idx = pl.program_id(1)
    b_idx = pl.program_id(0)
    gate_res = mod_ref[b_idx, 2:3, :]

    def norm_mod(xv):
        return ((_rms_scale(xv) * n1_ref[0]) * (1.0 + mod_ref[b_idx, 1:2, :])
                + mod_ref[b_idx, 0:1, :]).astype(BF16)

    @pl.when(s_idx == 0)
    def _():
        state_s[...] = jnp.zeros_like(state_s)
        pbuf_s[0:POOL_HALO, :] = jnp.zeros((POOL_HALO, POOL_DIM), F32)
        h_s[...] = norm_mod(x_ref[0])

    @pl.when(s_idx != 0)
    def _():
        h_s[...] = hn_s[...]

    row0 = pl.multiple_of(s_idx * rows, rows)
    cos = cos_ref[pl.ds(row0, rows), :]
    sin = sin_ref[pl.ds(row0, rows), :]
    x1_lane = (lax.broadcasted_iota(jnp.int32, (rows, LANES), 1) & HALF) == 0

    def rotary(t):
        partner = jnp.where(x1_lane, pltpu.roll(t, LANES - HALF, axis=1), pltpu.roll(t, HALF, axis=1))
        return t * cos + partner * sin

    for n in range(2 * RET_QK // MXU_N):
        cs = slice(O_QK + n * MXU_N, O_QK + (n + 1) * MXU_N)
        qk = _dot(h_s[...], wi_ref[0, :, cs])
        dst = q_s if n < RET_QK // MXU_N else k_s
        for j in range(MXU_N // LANES):
            col = (n % (RET_QK // MXU_N)) * MXU_N + j * LANES
            dst[:, col:col + LANES] = rotary(qk[:, j * LANES:(j + 1) * LANES])

    for n in range(RET_V // MXU_N):
        cs = slice(n * MXU_N, (n + 1) * MXU_N)
        cw = slice(O_V + n * MXU_N, O_V + (n + 1) * MXU_N)
        v_s[:, cs] = _dot(h_s[...], wi_ref[0, :, cw]).astype(BF16)

    lane = lax.broadcasted_iota(jnp.int32, (CHUNK, LANES), 1)
    even_lane = lane < RET_QK_DIM
    row_odd = lax.broadcasted_iota(jnp.int32, (LANES, 2 * RET_V_DIM), 0) >= RET_QK_DIM
    col_odd = lax.broadcasted_iota(jnp.int32, (LANES, 2 * RET_V_DIM), 1) >= RET_V_DIM
    same_head = row_odd == col_odd
    zero_v = jnp.zeros((CHUNK, RET_V_DIM), BF16)
    for p in range(HEAD_PAIRS):
        sl = slice(p * LANES, (p + 1) * LANES)
        vs = slice(p * 2 * RET_V_DIM, (p + 1) * 2 * RET_V_DIM)
        st = state_s[p]
        for c in range(rows // CHUNK):
            rs = slice(c * CHUNK, (c + 1) * CHUNK)
            qc = q_s[rs, sl]
            kc = k_s[rs, sl]
            v_pair = v_s[rs, vs]
            k_cat = jnp.concatenate([jnp.where(even_lane, kc, 0.0).astype(BF16),
                                     jnp.where(even_lane, 0.0, kc).astype(BF16)], axis=0)
            scores = lax.dot_general(qc.astype(BF16), k_cat, (((1,), (1,)), ((), ())),
                                     preferred_element_type=F32)
            sb = (scores * dtab_ref[p]).astype(BF16)
            v_diag = jnp.concatenate(
                [jnp.concatenate([v_pair[:, :RET_V_DIM], zero_v], axis=1),
                 jnp.concatenate([zero_v, v_pair[:, RET_V_DIM:]], axis=1)], axis=0)
            qd = (qc * qd_ref[p]).astype(BF16)
            out = _dot(sb, v_diag) + _dot(qd, st.astype(BF16))
            kd = (kc * kd_ref[p]).astype(BF16)
            kv = lax.dot_general(kd, v_pair, (((0,), (0,)), ((), ())),
                                 preferred_element_type=F32)
            st = st * cd_ref[p] + jnp.where(same_head, kv, 0.0)
            for hh in range(2):
                hs = slice((2 * p + hh) * RET_V_DIM, (2 * p + hh + 1) * RET_V_DIM)
                ret_s[rs, hs] = _rms_scale(out[:, hh * RET_V_DIM:(hh + 1) * RET_V_DIM])
        state_s[p] = st

    for n in range(RET_V // MXU_N):
        cs = slice(n * MXU_N, (n + 1) * MXU_N)
        g = _dot(h_s[...], wi_ref[0, :, O_GSW + n * MXU_N:O_GSW + (n + 1) * MXU_N])
        ra_s[:, cs] = (g * _sigmoid(g) * ret_s[:, cs]).astype(BF16)

    pbuf_s[POOL_HALO:POOL_HALO + rows, :] = _dot(h_s[...], wi_ref[0, :, O_POOL:O_GATES])
    head_pos = s_idx * rows + lax.broadcasted_iota(jnp.int32, (POOL_HALO, POOL_GROUP_DIM), 0)
    pooled = []
    for g, w in enumerate(POOL_WINDOWS):
        cs = slice(g * POOL_GROUP_DIM, (g + 1) * POOL_GROUP_DIM)
        ext = pbuf_s[:, cs]
        cur = ext[POOL_HALO:]
        span = 1
        while span < w:
            ext = ext + pltpu.roll(ext, span, axis=0)
            span *= 2
        acc = ext[POOL_HALO:]
        head_inv = 1.0 / jnp.minimum(head_pos + 1, w).astype(F32)
        mean = jnp.concatenate([acc[:POOL_HALO] * head_inv, acc[POOL_HALO:] * (1.0 / w)], axis=0)
        pooled.append((mean - cur).astype(BF16))
    pbuf_s[0:POOL_HALO, :] = pbuf_s[rows:rows + POOL_HALO, :]
    for gp in range(len(POOL_WINDOWS) // 2):
        cs = slice(gp * MXU_N, (gp + 1) * MXU_N)
        pair = jnp.concatenate([pooled[2 * gp], pooled[2 * gp + 1]], axis=1)
        y = _dot(pair, wpg_ref[0, gp]) * ps_ref[0, :, cs]
        pa_s[:, cs] = y.astype(BF16)

    for n in range(D_MODEL // MXU_N):
        cs = slice(n * MXU_N, (n + 1) * MXU_N)
        a_ret = _dot(h_s[...], wi_ref[0, :, O_GATES + n * MXU_N:O_GATES + (n + 1) * MXU_N])
        a_pool = _dot(h_s[...], wi_ref[0, :, O_GATES + D_MODEL + n * MXU_N:
                                        O_GATES + D_MODEL + (n + 1) * MXU_N])
        ret_d = _dot(ra_s[...], wro_ref[0, :, cs])
        pool_d = _dot(pa_s[...], wpo_ref[0, :, cs])
        mg_s[:, cs] = (_sigmoid(a_ret) * ret_d + _sigmoid(a_pool) * pool_d).astype(BF16)

    n_out = D_MODEL // MXU_N
    part = rows // n_out
    for n in range(n_out):
        cs = slice(n * MXU_N, (n + 1) * MXU_N)
        rs = slice(n * part, (n + 1) * part)
        hb = norm_mod(xn_ref[0, rs, :])
        hn_s[rs, :] = hb
        y = _with_token(_dot(mg_s[...], wout_ref[0, :, cs]), _zero_token(hb))
        o_ref[0, :, cs] = x_ref[0, :, cs] + gate_res[:, cs] * y


def _mixer_call(layer, x, mod, norm1, cos, sin, wts, tabs):
    bsz, seq, d = x.shape
    rows = MIX_ROWS
    wi, wro, wpg, ps, wpo, wout = wts
    dtab, qd, kd, cd = tabs

    def lw(arr):
        shape = (1,) + arr.shape[1:]
        zeros = (0,) * (arr.ndim - 1)
        return _resident(shape, lambda b, s: (layer,) + zeros)

    def const(arr):
        zeros = (0,) * arr.ndim
        return _resident(arr.shape, lambda b, s: zeros)

    last = seq // rows - 1
    in_specs = [
        pl.BlockSpec((1, rows, d), lambda b, s: (b, s, 0)),
        pl.BlockSpec((1, rows, d), lambda b, s: (b, jnp.minimum(s + 1, last), 0)),
        const(mod),
        lw(norm1),
        const(cos), const(sin),
        lw(wi), lw(wro), lw(wpg), lw(ps), lw(wpo), lw(wout),
        const(dtab), const(qd), const(kd), const(cd),
    ]
    scratch = [
        pltpu.VMEM((rows, d), BF16),
        pltpu.VMEM((rows, d), BF16),
        pltpu.VMEM((rows, RET_QK), F32),
        pltpu.VMEM((rows, RET_QK), F32),
        pltpu.VMEM((rows, RET_V), BF16),
        pltpu.VMEM((rows, RET_V), F32),
        pltpu.VMEM((rows, RET_V), BF16),
        pltpu.VMEM((rows + POOL_HALO, POOL_DIM), F32),
        pltpu.VMEM((rows, POOL_DIM), BF16),
        pltpu.VMEM((rows, d), BF16),
        pltpu.VMEM((HEAD_PAIRS, LANES, 2 * RET_V_DIM), F32),
    ]
    return pl.pallas_call(
        _mixer_kernel,
        grid=(bsz, seq // rows),
        in_specs=in_specs,
        out_specs=pl.BlockSpec((1, rows, d), lambda b, s: (b, s, 0)),
        out_shape=jax.ShapeDtypeStruct(x.shape, F32),
        scratch_shapes=scratch,
        compiler_params=pltpu.CompilerParams(
            dimension_semantics=("arbitrary", "arbitrary"),
            vmem_limit_bytes=VMEM_LIMIT),
        name=f"mixer_l{layer}",
    )(x, x, mod, norm1, cos, sin, wi, wro, wpg, ps, wpo, wout,
      dtab, qd, kd, cd)


def _ffn_kernel(x_ref, xn_ref, mod_ref, n2_ref, wi_ref, wo_ref, fn_ref, o_ref,
                h_s, hn_s, act_s, *, final, tiles_per_seq):
    n_tiles = pl.num_programs(0)
    b_idx = pl.program_id(0) // tiles_per_seq
    bn_idx = jnp.minimum(pl.program_id(0) + 1, n_tiles - 1) // tiles_per_seq
    gate_res = mod_ref[b_idx, 5:6, :]

    def norm_mod(xv, b):
        return ((_rms_scale(xv) * n2_ref[0]) * (1.0 + mod_ref[b, 4:5, :]) + mod_ref[b, 3:4, :]).astype(BF16)

    @pl.when(pl.program_id(0) == 0)
    def _():
        h_s[...] = norm_mod(x_ref[...], b_idx)

    @pl.when(pl.program_id(0) != 0)
    def _():
        h_s[...] = hn_s[...]

    n_chunks = D_FF // MXU_N
    n_parts = 4
    part = x_ref.shape[0] // n_parts
    for j in range(n_chunks):
        cs = slice(j * MXU_N, (j + 1) * MXU_N)
        cs2 = slice(D_FF + j * MXU_N, D_FF + (j + 1) * MXU_N)
        gate = _dot(h_s[...], wi_ref[0, :, cs])
        up = _dot(h_s[...], wi_ref[0, :, cs2])
        act = gate * _sigmoid(gate) * up
        if j % 3 == 1 and j // 3 < n_parts:
            rs = slice((j // 3) * part, (j // 3 + 1) * part)
            hb = norm_mod(xn_ref[rs, :], bn_idx)
            hn_s[rs, :] = hb
            act = _with_token(act, _zero_token(hb))
        act_s[:, cs] = act.astype(BF16)

    for n in range(D_MODEL // MXU_N):
        cs = slice(n * MXU_N, (n + 1) * MXU_N)
        y = _dot(act_s[...], wo_ref[0, :, cs])
        o_ref[:, cs] = x_ref[:, cs] + gate_res[:, cs] * y

    if final:
        o_ref[...] = _rms_scale(o_ref[...]) * fn_ref[...]


def _ffn_call(layer, x2d, mod, norm2, wi, wo, final_norm, seq, final):
    n_rows, d = x2d.shape
    rows = FFN_ROWS
    tiles_per_seq = seq // rows
    last = n_rows // rows - 1

    return pl.pallas_call(
        functools.partial(_ffn_kernel, final=final, tiles_per_seq=tiles_per_seq),
        grid=(n_rows // rows,),
        in_specs=[
            pl.BlockSpec((rows, d), lambda i: (i, 0)),
            pl.BlockSpec((rows, d), lambda i: (jnp.minimum(i + 1, last), 0)),
            _resident(mod.shape, lambda i: (0, 0, 0)),
            _resident((1, 1, d), lambda i: (layer, 0, 0)),
            _resident((1, d, 2 * D_FF), lambda i: (layer, 0, 0)),
            _resident((1, D_FF, d), lambda i: (layer, 0, 0)),
            _resident((1, d), lambda i: (0, 0)),
        ],
        out_specs=pl.BlockSpec((rows, d), lambda i: (i, 0)),
        out_shape=jax.ShapeDtypeStruct(x2d.shape, F32),
        scratch_shapes=[
            pltpu.VMEM((rows, d), BF16),
            pltpu.VMEM((rows, d), BF16),
            pltpu.VMEM((rows, D_FF), BF16),
        ],
        compiler_params=pltpu.CompilerParams(
            dimension_semantics=("arbitrary",),
            vmem_limit_bytes=VMEM_LIMIT),
        name=f"swiglu_l{layer}",
    )(x2d, x2d, mod, norm2, wi, wo, final_norm)


def _retention_tables():
    heads = jnp.arange(RET_HEADS, dtype=F32)
    log_g = jnp.log1p(-(2.0 ** (-5.0 - heads)))
    idx = jnp.arange(CHUNK, dtype=F32)
    diff = idx[:, None] - idx[None, :]
    qscale = RET_QK_DIM ** -0.5
    intra = jnp.where(diff >= 0, jnp.exp(log_g[:, None, None] * jnp.maximum(diff, 0.0)), 0.0)
    dtab = intra * qscale
    q_dec = jnp.exp(log_g[:, None] * (idx + 1.0)) * qscale
    k_dec = jnp.exp(log_g[:, None] * (CHUNK - 1.0 - idx))
    chunk_dec = jnp.exp(log_g * CHUNK)
    parity = jnp.arange(LANES) // RET_QK_DIM
    pair_head = 2 * jnp.arange(HEAD_PAIRS)[:, None] + parity[None, :]
    dtab_p = jnp.concatenate([dtab[0::2], dtab[1::2]], axis=2)
    qd_p = jnp.transpose(q_dec[pair_head], (0, 2, 1))
    kd_p = jnp.transpose(k_dec[pair_head], (0, 2, 1))
    cd_p = jnp.concatenate(
        [jnp.broadcast_to(chunk_dec[0::2, None, None], (HEAD_PAIRS, 1, RET_V_DIM)),
         jnp.broadcast_to(chunk_dec[1::2, None, None], (HEAD_PAIRS, 1, RET_V_DIM))], axis=2)
    return dtab_p, qd_p, kd_p, cd_p


def _rotary_tables(seq):
    inv_freq = ROPE_BASE ** (-jnp.arange(HALF, dtype=F32) / HALF)
    ang = jnp.arange(seq, dtype=F32)[:, None] * inv_freq[None, :]
    cos = jnp.cos(ang)
    sin = jnp.sin(ang)
    cos4 = jnp.concatenate([cos, cos, cos, cos], axis=-1)
    sin4 = jnp.concatenate([-sin, sin, -sin, sin], axis=-1)
    return cos4, sin4


def kernel(x, c, w_ada, b_ada, norm1, w_in, w_ret_o, w_pool_grp, pool_scale, w_pool_o,
           w_out, norm2, w_ffn_in, w_ffn_out, final_norm):
    bsz, seq, d = x.shape
    depth = w_in.shape[0]
    assert seq % MIX_ROWS == 0 and seq % FFN_ROWS == 0 and MIX_ROWS % CHUNK == 0

    mod = _ada_call(c, w_ada, b_ada).reshape(depth, bsz, N_MOD, d)

    assert w_in.shape[2] == IN_COLS
    n_gp = len(POOL_WINDOWS) // 2
    wpg = w_pool_grp.astype(BF16).reshape(depth, n_gp, 2, POOL_GROUP_DIM, POOL_GROUP_DIM)
    wpg = jnp.einsum('lpkio,kq->lpkiqo', wpg, jnp.eye(2, dtype=BF16))
    wpg = wpg.reshape(depth, n_gp, 2 * POOL_GROUP_DIM, 2 * POOL_GROUP_DIM)
    wts = (w_in.astype(BF16), w_ret_o.astype(BF16), wpg,
           pool_scale.reshape(depth, 1, POOL_DIM), w_pool_o.astype(BF16), w_out.astype(BF16))
    wi = w_ffn_in.astype(BF16)
    wo = w_ffn_out.astype(BF16)
    n1 = norm1.reshape(depth, 1, d)
    n2 = norm2.reshape(depth, 1, d)
    fn = final_norm.reshape(1, d)

    tabs = _retention_tables()
    cos4, sin4 = _rotary_tables(seq)

    for l in range(depth):
        x = _mixer_call(l, x, mod[l], n1, cos4, sin4, wts, tabs)
        x2d = _ffn_call(l, x.reshape(bsz * seq, d), mod[l], n2, wi, wo, fn, seq,
                        final=(l == depth - 1))
        x = x2d.reshape(bsz, seq, d)
    return x
```

```python
import functools

import jax
import jax.numpy as jnp
from jax import lax
from jax.experimental import pallas as pl
from jax.experimental.pallas import tpu as pltpu

D_MODEL = 1024
RET_HEADS = 8
RET_QK_DIM = 64
RET_V_DIM = 128
RET_QK = RET_HEADS * RET_QK_DIM
RET_V = RET_HEADS * RET_V_DIM
CHUNK = 128
ROPE_BASE = 10000.0
POOL_WINDOWS = (2, 4, 8, 16)
POOL_DIM = 512
POOL_GROUP_DIM = 128
POOL_HALO = 16
D_FF = 2816
N_MOD = 6
EPS = 1e-6

LANES = 128
SUBLANES = 8
MXU_N = 256
HALF = RET_QK_DIM // 2
HEAD_PAIRS = RET_HEADS // 2
O_QK = 0
O_V = 2 * RET_QK
O_GSW = O_V + RET_V
O_POOL = O_GSW + RET_V
O_GATES = O_POOL + POOL_DIM
IN_COLS = O_GATES + 2 * D_MODEL

MIX_ROWS = 512
NEXT_TILE_PARTS = 16
FFN_ROWS = 512
VMEM_LIMIT = 56 * 1024 * 1024

F32 = jnp.float32
BF16 = jnp.bfloat16


def _dot(a, b):
    return jnp.dot(a, b, preferred_element_type=F32)


def _sigmoid(x):
    return 1.0 / (1.0 + jnp.exp(-x))


def _rms_scale(x):
    return x * lax.rsqrt(jnp.mean(x * x, axis=-1, keepdims=True) + EPS)


def _zero_token(v):
    bits = pltpu.bitcast(v, jnp.int32)
    n_rows, n_cols = bits.shape
    acc = bits[0:SUBLANES]
    for i in range(1, n_rows // SUBLANES):
        acc = acc | bits[i * SUBLANES:(i + 1) * SUBLANES]
    tok = acc[:, 0:LANES]
    for j in range(1, n_cols // LANES):
        tok = tok | acc[:, j * LANES:(j + 1) * LANES]
    tok = lax.shift_right_logical(lax.shift_right_logical(tok, 16), 16)
    return tok.astype(F32)


def _with_token(val, tok):
    head = val[0:SUBLANES] + jnp.concatenate([tok] * (val.shape[1] // LANES), axis=1)
    return jnp.concatenate([head, val[SUBLANES:]], axis=0)


def _resident(shape, index_map):
    return pl.BlockSpec(shape, index_map, pipeline_mode=pl.Buffered(1))


def _ada_kernel(c_ref, w_ref, b_ref, o_ref):
    c = c_ref[...]
    ca = (c * _sigmoid(c)).astype(BF16)
    o_ref[0] = _dot(ca, w_ref[0].astype(BF16)) + b_ref[0]


def _ada_call(c, w_ada, b_ada):
    depth, d, n = w_ada.shape
    bsz = c.shape[0]
    bn = 1024
    return pl.pallas_call(
        _ada_kernel,
        grid=(depth, n // bn),
        in_specs=[
            pl.BlockSpec((bsz, d), lambda l, j: (0, 0)),
            pl.BlockSpec((1, d, bn), lambda l, j: (l, 0, j)),
            pl.BlockSpec((1, 1, bn), lambda l, j: (l, 0, j)),
        ],
        out_specs=pl.BlockSpec((1, bsz, bn), lambda l, j: (l, 0, j)),
        out_shape=jax.ShapeDtypeStruct((depth, bsz, n), F32),
        compiler_params=pltpu.CompilerParams(
            dimension_semantics=("arbitrary", "arbitrary"),
            vmem_limit_bytes=VMEM_LIMIT),
        name="adaln_mod",
    )(c, w_ada, b_ada.reshape(depth, 1, n))


def _mixer_kernel(x_ref, xn_ref, mod_ref, n1_ref, cos_ref, sin_ref,
                  wi_ref, wro_ref, wpg_ref, ps_ref, wpo_ref, wout_ref,
                  dtab_ref, qd_ref, kd_ref, cd_ref,
                  o_ref,
                  h_s, hn_s, q_s, k_s, v_s, ret_s, ra_s, pbuf_s, pa_s, mg_s, state_s):
    rows = x_ref.shape[1]
    s_idx = pl.program_id(1)
    b_idx = pl.program_id(0)
    n_s = pl.num_programs(1)
    bn_idx = jnp.minimum(b_idx * n_s + s_idx + 1, pl.num_programs(0) * n_s - 1) // n_s
    gate_res = mod_ref[b_idx, 2:3, :]

    def norm_mod(xv, b):
        return ((_rms_scale(xv) * n1_ref[0]) * (1.0 + mod_ref[b, 1:2, :]) + mod_ref[b, 0:1, :]).astype(BF16)

    @pl.when(s_idx == 0)
    def _():
        state_s[...] = jnp.zeros_like(state_s)
        pbuf_s[0:POOL_HALO, :] = jnp.zeros((POOL_HALO, POOL_DIM), F32)

    @pl.when((s_idx == 0) & (b_idx == 0))
    def _():
        hn_s[...] = norm_mod(x_ref[0], b_idx)

    part = rows // NEXT_TILE_PARTS

    def next_tile_parts(lo, hi):
        tok = None
        for i in range(lo, hi):
            rs = slice(i * part, (i + 1) * part)
            hb = norm_mod(xn_ref[0, rs, :], bn_idx)
            hn_s[rs, :] = hb
            tok = _zero_token(hb) if tok is None else tok + _zero_token(hb)
        return tok

    row0 = pl.multiple_of(s_idx * rows, rows)
    cos = cos_ref[pl.ds(row0, rows), :]
    sin = sin_ref[pl.ds(row0, rows), :]
    x1_lane = (lax.broadcasted_iota(jnp.int32, (rows, LANES), 1) & HALF) == 0

    def rotary(t):
        partner = jnp.where(x1_lane, pltpu.roll(t, LANES - HALF, axis=1), pltpu.roll(t, HALF, axis=1))
        return t * cos + partner * sin

    for n in range(2 * RET_QK // MXU_N):
        cs = slice(O_QK + n * MXU_N, O_QK + (n + 1) * MXU_N)
        qk = _dot(hn_s[...], wi_ref[0, :, cs])
        dst = q_s if n < RET_QK // MXU_N else k_s
        for j in range(MXU_N // LANES):
            col = (n % (RET_QK // MXU_N)) * MXU_N + j * LANES
            dst[:, col:col + LANES] = rotary(qk[:, j * LANES:(j + 1) * LANES])
    h_s[...] = hn_s[...]

    for n in range(RET_V // MXU_N):
        cs = slice(n * MXU_N, (n + 1) * MXU_N)
        cw = slice(O_V + n * MXU_N, O_V + (n + 1) * MXU_N)
        v = _with_token(_dot(h_s[...], wi_ref[0, :, cw]), next_tile_parts(12 + n, 13 + n))
        v_s[:, cs] = v.astype(BF16)

    lane = lax.broadcasted_iota(jnp.int32, (CHUNK, LANES), 1)
    even_lane = lane < RET_QK_DIM
    row_odd = lax.broadcasted_iota(jnp.int32, (LANES, 2 * RET_V_DIM), 0) >= RET_QK_DIM
    col_odd = lax.broadcasted_iota(jnp.int32, (LANES, 2 * RET_V_DIM), 1) >= RET_V_DIM
    same_head = row_odd == col_odd
    zero_v = jnp.zeros((CHUNK, RET_V_DIM), BF16)
    for p in range(HEAD_PAIRS):
        sl = slice(p * LANES, (p + 1) * LANES)
        vs = slice(p * 2 * RET_V_DIM, (p + 1) * 2 * RET_V_DIM)
        st = state_s[p]
        for c in range(rows // CHUNK):
            rs = slice(c * CHUNK, (c + 1) * CHUNK)
            qc = q_s[rs, sl]
            kc = k_s[rs, sl]
            v_pair = v_s[rs, vs]
            k_cat = jnp.concatenate([jnp.where(even_lane, kc, 0.0).astype(BF16),
                                     jnp.where(even_lane, 0.0, kc).astype(BF16)], axis=0)
            scores = lax.dot_general(qc.astype(BF16), k_cat, (((1,), (1,)), ((), ())),
                                     preferred_element_type=F32)
            sb = (scores * dtab_ref[p]).astype(BF16)
            v_diag = jnp.concatenate(
                [jnp.concatenate([v_pair[:, :RET_V_DIM], zero_v], axis=1),
                 jnp.concatenate([zero_v, v_pair[:, RET_V_DIM:]], axis=1)], axis=0)
            qd = (qc * qd_ref[p]).astype(BF16)
            out = _dot(sb, v_diag) + _dot(qd, st.astype(BF16))
            kd = (kc * kd_ref[p]).astype(BF16)
            kv = lax.dot_general(kd, v_pair, (((0,), (0,)), ((), ())),
                                 preferred_element_type=F32)
            st = st * cd_ref[p] + jnp.where(same_head, kv, 0.0)
            for hh in range(2):
                hs = slice((2 * p + hh) * RET_V_DIM, (2 * p + hh + 1) * RET_V_DIM)
                ret_s[rs, hs] = _rms_scale(out[:, hh * RET_V_DIM:(hh + 1) * RET_V_DIM])
        state_s[p] = st

    for n in range(RET_V // MXU_N):
        cs = slice(n * MXU_N, (n + 1) * MXU_N)
        g = _dot(h_s[...], wi_ref[0, :, O_GSW + n * MXU_N:O_GSW + (n + 1) * MXU_N])
        gated = _with_token(g * _sigmoid(g) * ret_s[:, cs], next_tile_parts(n, n + 1))
        ra_s[:, cs] = gated.astype(BF16)

    pbuf_s[POOL_HALO:POOL_HALO + rows, :] = _dot(h_s[...], wi_ref[0, :, O_POOL:O_GATES])
    head_pos = s_idx * rows + lax.broadcasted_iota(jnp.int32, (POOL_HALO, POOL_GROUP_DIM), 0)
    pooled = []
    for g, w in enumerate(POOL_WINDOWS):
        cs = slice(g * POOL_GROUP_DIM, (g + 1) * POOL_GROUP_DIM)
        ext = pbuf_s[:, cs]
        cur = ext[POOL_HALO:]
        span = 1
        while span < w:
            ext = ext + pltpu.roll(ext, span, axis=0)
            span *= 2
        acc = ext[POOL_HALO:]
        head_inv = 1.0 / jnp.minimum(head_pos + 1, w).astype(F32)
        mean = jnp.concatenate([acc[:POOL_HALO] * head_inv, acc[POOL_HALO:] * (1.0 / w)], axis=0)
        pooled.append((mean - cur).astype(BF16))
    pbuf_s[0:POOL_HALO, :] = pbuf_s[rows:rows + POOL_HALO, :]
    for gp in range(len(POOL_WINDOWS) // 2):
        cs = slice(gp * MXU_N, (gp + 1) * MXU_N)
        pair = jnp.concatenate([pooled[2 * gp], pooled[2 * gp + 1]], axis=1)
        y = _dot(pair, wpg_ref[0, gp]) * ps_ref[0, :, cs]
        pa_s[:, cs] = y.astype(BF16)

    n_out = D_MODEL // MXU_N
    for n in range(n_out):
        cs = slice(n * MXU_N, (n + 1) * MXU_N)
        a_ret = _dot(h_s[...], wi_ref[0, :, O_GATES + n * MXU_N:O_GATES + (n + 1) * MXU_N])
        a_pool = _dot(h_s[...], wi_ref[0, :, O_GATES + D_MODEL + n * MXU_N:
                                        O_GATES + D_MODEL + (n + 1) * MXU_N])
        ret_d = _dot(ra_s[...], wro_ref[0, :, cs])
        pool_d = _dot(pa_s[...], wpo_ref[0, :, cs])
        merged = _sigmoid(a_ret) * ret_d + _sigmoid(a_pool) * pool_d
        mg_s[:, cs] = _with_token(merged, next_tile_parts(4 + n, 5 + n)).astype(BF16)

    for n in range(n_out):
        cs = slice(n * MXU_N, (n + 1) * MXU_N)
        y = _with_token(_dot(mg_s[...], wout_ref[0, :, cs]), next_tile_parts(8 + n, 9 + n))
        o_ref[0, :, cs] = x_ref[0, :, cs] + gate_res[:, cs] * y


def _mixer_call(layer, x, mod, norm1, cos, sin, wts, tabs):
    bsz, seq, d = x.shape
    rows = MIX_ROWS
    wi, wro, wpg, ps, wpo, wout = wts
    dtab, qd, kd, cd = tabs

    def lw(arr):
        shape = (1,) + arr.shape[1:]
        zeros = (0,) * (arr.ndim - 1)
        return _resident(shape, lambda b, s: (layer,) + zeros)

    def const(arr):
        zeros = (0,) * arr.ndim
        return _resident(arr.shape, lambda b, s: zeros)

    n_s = seq // rows

    def next_tile(b, s):
        flat = jnp.minimum(b * n_s + s + 1, bsz * n_s - 1)
        return (flat // n_s, flat % n_s, 0)

    in_specs = [
        pl.BlockSpec((1, rows, d), lambda b, s: (b, s, 0)),
        pl.BlockSpec((1, rows, d), next_tile),
        const(mod),
        lw(norm1),
        const(cos), const(sin),
        lw(wi), lw(wro), lw(wpg), lw(ps), lw(wpo), lw(wout),
        const(dtab), const(qd), const(kd), const(cd),
    ]
    scratch = [
        pltpu.VMEM((rows, d), BF16),
        pltpu.VMEM((rows, d), BF16),
        pltpu.VMEM((rows, RET_QK), F32),
        pltpu.VMEM((rows, RET_QK), F32),
        pltpu.VMEM((rows, RET_V), BF16),
        pltpu.VMEM((rows, RET_V), F32),
        pltpu.VMEM((rows, RET_V), BF16),
        pltpu.VMEM((rows + POOL_HALO, POOL_DIM), F32),
        pltpu.VMEM((rows, POOL_DIM), BF16),
        pltpu.VMEM((rows, d), BF16),
        pltpu.VMEM((HEAD_PAIRS, LANES, 2 * RET_V_DIM), F32),
    ]
    return pl.pallas_call(
        _mixer_kernel,
        grid=(bsz, seq // rows),
        in_specs=in_specs,
        out_specs=pl.BlockSpec((1, rows, d), lambda b, s: (b, s, 0)),
        out_shape=jax.ShapeDtypeStruct(x.shape, F32),
        scratch_shapes=scratch,
        compiler_params=pltpu.CompilerParams(
            dimension_semantics=("arbitrary", "arbitrary"),
            vmem_limit_bytes=VMEM_LIMIT),
        name=f"mixer_l{layer}",
    )(x, x, mod, norm1, cos, sin, wi, wro, wpg, ps, wpo, wout,
      dtab, qd, kd, cd)


def _ffn_kernel(x_ref, xn_ref, mod_ref, n2_ref, wi_ref, wo_ref, fn_ref, o_ref,
                h_s, hn_s, act_s, *, final, tiles_per_seq):
    n_tiles = pl.num_programs(0)
    b_idx = pl.program_id(0) // tiles_per_seq
    bn_idx = jnp.minimum(pl.program_id(0) + 1, n_tiles - 1) // tiles_per_seq
    gate_res = mod_ref[b_idx, 5:6, :]

    def norm_mod(xv, b):
        return ((_rms_scale(xv) * n2_ref[0]) * (1.0 + mod_ref[b, 4:5, :]) + mod_ref[b, 3:4, :]).astype(BF16)

    @pl.when(pl.program_id(0) == 0)
    def _():
        hn_s[...] = norm_mod(x_ref[...], b_idx)

    n_chunks = D_FF // MXU_N
    n_parts = 4
    part = x_ref.shape[0] // n_parts
    for j in range(n_chunks):
        cs = slice(j * MXU_N, (j + 1) * MXU_N)
        cs2 = slice(D_FF + j * MXU_N, D_FF + (j + 1) * MXU_N)
        src = hn_s if j == 0 else h_s
        gate = _dot(src[...], wi_ref[0, :, cs])
        up = _dot(src[...], wi_ref[0, :, cs2])
        if j == 0:
            h_s[...] = hn_s[...]
        act = gate * _sigmoid(gate) * up
        if j % 3 == 1 and j // 3 < n_parts:
            rs = slice((j // 3) * part, (j // 3 + 1) * part)
            hb = norm_mod(xn_ref[rs, :], bn_idx)
            hn_s[rs, :] = hb
            act = _with_token(act, _zero_token(hb))
        act_s[:, cs] = act.astype(BF16)

    for n in range(D_MODEL // MXU_N):
        cs = slice(n * MXU_N, (n + 1) * MXU_N)
        y = _dot(act_s[...], wo_ref[0, :, cs])
        o_ref[:, cs] = x_ref[:, cs] + gate_res[:, cs] * y

    if final:
        o_ref[...] = _rms_scale(o_ref[...]) * fn_ref[...]


def _ffn_call(layer, x2d, mod, norm2, wi, wo, final_norm, seq, final):
    n_rows, d = x2d.shape
    rows = FFN_ROWS
    tiles_per_seq = seq // rows
    last = n_rows // rows - 1

    return pl.pallas_call(
        functools.partial(_ffn_kernel, final=final, tiles_per_seq=tiles_per_seq),
        grid=(n_rows // rows,),
        in_specs=[
            pl.BlockSpec((rows, d), lambda i: (i, 0)),
            pl.BlockSpec((rows, d), lambda i: (jnp.minimum(i + 1, last), 0)),
            _resident(mod.shape, lambda i: (0, 0, 0)),
            _resident((1, 1, d), lambda i: (layer, 0, 0)),
            _resident((1, d, 2 * D_FF), lambda i: (layer, 0, 0)),
            _resident((1, D_FF, d), lambda i: (layer, 0, 0)),
            _resident((1, d), lambda i: (0, 0)),
        ],
        out_specs=pl.BlockSpec((rows, d), lambda i: (i, 0)),
        out_shape=jax.ShapeDtypeStruct(x2d.shape, F32),
        scratch_shapes=[
            pltpu.VMEM((rows, d), BF16),
            pltpu.VMEM((rows, d), BF16),
            pltpu.VMEM((rows, D_FF), BF16),
        ],
        compiler_params=pltpu.CompilerParams(
            dimension_semantics=("arbitrary",),
            vmem_limit_bytes=VMEM_LIMIT),
        name=f"swiglu_l{layer}",
    )(x2d, x2d, mod, norm2, wi, wo, final_norm)


def _retention_tables():
    heads = jnp.arange(RET_HEADS, dtype=F32)
    log_g = jnp.log1p(-(2.0 ** (-5.0 - heads)))
    idx = jnp.arange(CHUNK, dtype=F32)
    diff = idx[:, None] - idx[None, :]
    qscale = RET_QK_DIM ** -0.5
    intra = jnp.where(diff >= 0, jnp.exp(log_g[:, None, None] * jnp.maximum(diff, 0.0)), 0.0)
    dtab = intra * qscale
    q_dec = jnp.exp(log_g[:, None] * (idx + 1.0)) * qscale
    k_dec = jnp.exp(log_g[:, None] * (CHUNK - 1.0 - idx))
    chunk_dec = jnp.exp(log_g * CHUNK)
    parity = jnp.arange(LANES) // RET_QK_DIM
    pair_head = 2 * jnp.arange(HEAD_PAIRS)[:, None] + parity[None, :]
    dtab_p = jnp.concatenate([dtab[0::2], dtab[1::2]], axis=2)
    qd_p = jnp.transpose(q_dec[pair_head], (0, 2, 1))
    kd_p = jnp.transpose(k_dec[pair_head], (0, 2, 1))
    cd_p = jnp.concatenate(
        [jnp.broadcast_to(chunk_dec[0::2, None, None], (HEAD_PAIRS, 1, RET_V_DIM)),
         jnp.broadcast_to(chunk_dec[1::2, None, None], (HEAD_PAIRS, 1, RET_V_DIM))], axis=2)
    return dtab_p, qd_p, kd_p, cd_p


def _rotary_tables(seq):
    inv_freq = ROPE_BASE ** (-jnp.arange(HALF, dtype=F32) / HALF)
    ang = jnp.arange(seq, dtype=F32)[:, None] * inv_freq[None, :]
    cos = jnp.cos(ang)
    sin = jnp.sin(ang)
    cos4 = jnp.concatenate([cos, cos, cos, cos], axis=-1)
    sin4 = jnp.concatenate([-sin, sin, -sin, sin], axis=-1)
    return cos4, sin4


def kernel(x, c, w_ada, b_ada, norm1, w_in, w_ret_o, w_pool_grp, pool_scale, w_pool_o,
           w_out, norm2, w_ffn_in, w_ffn_out, final_norm):
    bsz, seq, d = x.shape
    depth = w_in.shape[0]
    assert seq % MIX_ROWS == 0 and seq % FFN_ROWS == 0 and MIX_ROWS % CHUNK == 0

    mod = _ada_call(c, w_ada, b_ada).reshape(depth, bsz, N_MOD, d)

    assert w_in.shape[2] == IN_COLS
    n_gp = len(POOL_WINDOWS) // 2
    wpg = w_pool_grp.astype(BF16).reshape(depth, n_gp, 2, POOL_GROUP_DIM, POOL_GROUP_DIM)
    wpg = jnp.einsum('lpkio,kq->lpkiqo', wpg, jnp.eye(2, dtype=BF16))
    wpg = wpg.reshape(depth, n_gp, 2 * POOL_GROUP_DIM, 2 * POOL_GROUP_DIM)
    wts = (w_in.astype(BF16), w_ret_o.astype(BF16), wpg,
           pool_scale.reshape(depth, 1, POOL_DIM), w_pool_o.astype(BF16), w_out.astype(BF16))
    wi = w_ffn_in.astype(BF16)
    wo = w_ffn_out.astype(BF16)
    n1 = norm1.reshape(depth, 1, d)
    n2 = norm2.reshape(depth, 1, d)
    fn = final_norm.reshape(1, d)

    tabs = _retention_tables()
    cos4, sin4 = _rotary_tables(seq)

    for l in range(depth):
        x = _mixer_call(l, x, mod[l], n1, cos4, sin4, wts, tabs)
        x2d = _ffn_call(l, x.reshape(bsz * seq, d), mod[l], n2, wi, wo, fn, seq,
                        final=(l == depth - 1))
        x = x2d.reshape(bsz, seq, d)
    return x
```

```python
import functools

import jax
import jax.numpy as jnp
from jax import lax
from jax.experimental import pallas as pl
from jax.experimental.pallas import tpu as pltpu

D_MODEL = 1024
RET_HEADS = 8
RET_QK_DIM = 64
RET_V_DIM = 128
RET_QK = RET_HEADS * RET_QK_DIM
RET_V = RET_HEADS * RET_V_DIM
CHUNK = 128
ROPE_BASE = 10000.0
POOL_WINDOWS = (2, 4, 8, 16)
POOL_DIM = 512
POOL_GROUP_DIM = 128
POOL_HALO = 16
D_FF = 2816
N_MOD = 6
EPS = 1e-6

LANES = 128
SUBLANES = 8
MXU_N = 256
HALF = RET_QK_DIM // 2
HEAD_PAIRS = RET_HEADS // 2
O_QK = 0
O_V = 2 * RET_QK
O_GSW = O_V + RET_V
O_POOL = O_GSW + RET_V
O_GATES = O_POOL + POOL_DIM
IN_COLS = O_GATES + 2 * D_MODEL

MIX_ROWS = 512
NEXT_TILE_PARTS = 16
FFN_ROWS = 512
VMEM_LIMIT = 56 * 1024 * 1024

F32 = jnp.float32
BF16 = jnp.bfloat16


def _dot(a, b):
    return jnp.dot(a, b, preferred_element_type=F32)


def _sigmoid(x):
    return 1.0 / (1.0 + jnp.exp(-x))


def _rms_scale(x):
    return x * lax.rsqrt(jnp.mean(x * x, axis=-1, keepdims=True) + EPS)


def _zero_token(v):
    bits = pltpu.bitcast(v, jnp.int32)
    n_rows, n_cols = bits.shape
    acc = bits[0:SUBLANES]
    for i in range(1, n_rows // SUBLANES):
        acc = acc | bits[i * SUBLANES:(i + 1) * SUBLANES]
    tok = acc[:, 0:LANES]
    for j in range(1, n_cols // LANES):
        tok = tok | acc[:, j * LANES:(j + 1) * LANES]
    tok = lax.shift_right_logical(lax.shift_right_logical(tok, 16), 16)
    return tok.astype(F32)


def _with_token(val, tok):
    head = val[0:SUBLANES] + jnp.concatenate([tok] * (val.shape[1] // LANES), axis=1)
    return jnp.concatenate([head, val[SUBLANES:]], axis=0)


def _resident(shape, index_map):
    return pl.BlockSpec(shape, index_map, pipeline_mode=pl.Buffered(1))


def _ada_kernel(c_ref, w_ref, b_ref, o_ref):
    c = c_ref[...]
    ca = (c * _sigmoid(c)).astype(BF16)
    o_ref[0] = _dot(ca, w_ref[0].astype(BF16)) + b_ref[0]


def _ada_call(c, w_ada, b_ada):
    depth, d, n = w_ada.shape
    bsz = c.shape[0]
    bn = 1024
    return pl.pallas_call(
        _ada_kernel,
        grid=(depth, n // bn),
        in_specs=[
            pl.BlockSpec((bsz, d), lambda l, j: (0, 0)),
            pl.BlockSpec((1, d, bn), lambda l, j: (l, 0, j)),
            pl.BlockSpec((1, 1, bn), lambda l, j: (l, 0, j)),
        ],
        out_specs=pl.BlockSpec((1, bsz, bn), lambda l, j: (l, 0, j)),
        out_shape=jax.ShapeDtypeStruct((depth, bsz, n), F32),
        compiler_params=pltpu.CompilerParams(
            dimension_semantics=("arbitrary", "arbitrary"),
            vmem_limit_bytes=VMEM_LIMIT),
        name="adaln_mod",
    )(c, w_ada, b_ada.reshape(depth, 1, n))


def _mixer_kernel(x_ref, xn_ref, mod_ref, n1_ref, cos_ref, sin_ref,
                  wi_ref, wro_ref, wpg_ref, ps_ref, wpo_ref, wout_ref,
                  dtab_ref, qd_ref, kd_ref, cd_ref,
                  o_ref,
                  h_s, hn_s, q_s, k_s, v_s, ret_s, ra_s, pbuf_s, pa_s, mg_s, state_s):
    rows = x_ref.shape[1]
    s_idx = pl.program_id(1)
    b_idx = pl.program_id(0)
    n_s = pl.num_programs(1)
    bn_idx = jnp.minimum(b_idx * n_s + s_idx + 1, pl.num_programs(0) * n_s - 1) // n_s
    gate_res = mod_ref[b_idx, 2:3, :]

    def norm_mod(xv, b):
        return ((_rms_scale(xv) * n1_ref[0]) * (1.0 + mod_ref[b, 1:2, :]) + mod_ref[b, 0:1, :]).astype(BF16)

    @pl.when(s_idx == 0)
    def _():
        state_s[...] = jnp.zeros_like(state_s)
        pbuf_s[0:POOL_HALO, :] = jnp.zeros((POOL_HALO, POOL_DIM), F32)

    @pl.when((s_idx == 0) & (b_idx == 0))
    def _():
        hn_s[...] = norm_mod(x_ref[0], b_idx)

    part = rows // NEXT_TILE_PARTS

    def next_tile_parts(lo, hi):
        tok = None
        for i in range(lo, hi):
            rs = slice(i * part, (i + 1) * part)
            hb = norm_mod(xn_ref[0, rs, :], bn_idx)
            hn_s[rs, :] = hb
            tok = _zero_token(hb) if tok is None else tok + _zero_token(hb)
        return tok

    row0 = pl.multiple_of(s_idx * rows, rows)
    cos = cos_ref[pl.ds(row0, rows), :]
    sin = sin_ref[pl.ds(row0, rows), :]
    x1_lane = (lax.broadcasted_iota(jnp.int32, (rows, LANES), 1) & HALF) == 0

    def rotary(t):
        partner = jnp.where(x1_lane, pltpu.roll(t, LANES - HALF, axis=1), pltpu.roll(t, HALF, axis=1))
        return t * cos + partner * sin

    for n in range(2 * RET_QK // MXU_N):
        cs = slice(O_QK + n * MXU_N, O_QK + (n + 1) * MXU_N)
        qk = _dot(hn_s[...], wi_ref[0, :, cs])
        dst = q_s if n < RET_QK // MXU_N else k_s
        for j in range(MXU_N // LANES):
            col = (n % (RET_QK // MXU_N)) * MXU_N + j * LANES
            dst[:, col:col + LANES] = rotary(qk[:, j * LANES:(j + 1) * LANES])
    h_s[...] = hn_s[...]

    for n in range(RET_V // MXU_N):
        cs = slice(n * MXU_N, (n + 1) * MXU_N)
        cw = slice(O_V + n * MXU_N, O_V + (n + 1) * MXU_N)
        v = _with_token(_dot(h_s[...], wi_ref[0, :, cw]), next_tile_parts(12 + n, 13 + n))
        v_s[:, cs] = v.astype(BF16)

    lane = lax.broadcasted_iota(jnp.int32, (CHUNK, LANES), 1)
    even_lane = lane < RET_QK_DIM
    row_odd = lax.broadcasted_iota(jnp.int32, (LANES, 2 * RET_V_DIM), 0) >= RET_QK_DIM
    col_odd = lax.broadcasted_iota(jnp.int32, (LANES, 2 * RET_V_DIM), 1) >= RET_V_DIM
    same_head = row_odd == col_odd
    zero_v = jnp.zeros((CHUNK, RET_V_DIM), BF16)
    states = [state_s[p] for p in range(HEAD_PAIRS)]
    for c in range(rows // CHUNK):
        rs = slice(c * CHUNK, (c + 1) * CHUNK)
        for p in range(HEAD_PAIRS):
            sl = slice(p * LANES, (p + 1) * LANES)
            vs = slice(p * 2 * RET_V_DIM, (p + 1) * 2 * RET_V_DIM)
            st = states[p]
            qc = q_s[rs, sl]
            kc = k_s[rs, sl]
            v_pair = v_s[rs, vs]
            k_cat = jnp.concatenate([jnp.where(even_lane, kc, 0.0).astype(BF16),
                                     jnp.where(even_lane, 0.0, kc).astype(BF16)], axis=0)
            scores = lax.dot_general(qc.astype(BF16), k_cat, (((1,), (1,)), ((), ())),
                                     preferred_element_type=F32)
            sb = (scores * dtab_ref[p]).astype(BF16)
            v_diag = jnp.concatenate(
                [jnp.concatenate([v_pair[:, :RET_V_DIM], zero_v], axis=1),
                 jnp.concatenate([zero_v, v_pair[:, RET_V_DIM:]], axis=1)], axis=0)
            qd = (qc * qd_ref[p]).astype(BF16)
            out = _dot(sb, v_diag) + _dot(qd, st.astype(BF16))
            kd = (kc * kd_ref[p]).astype(BF16)
            kv = lax.dot_general(kd, v_pair, (((0,), (0,)), ((), ())),
                                 preferred_element_type=F32)
            states[p] = st * cd_ref[p] + jnp.where(same_head, kv, 0.0)
            for hh in range(2):
                hs = slice((2 * p + hh) * RET_V_DIM, (2 * p + hh + 1) * RET_V_DIM)
                ret_s[rs, hs] = _rms_scale(out[:, hh * RET_V_DIM:(hh + 1) * RET_V_DIM])
    for p in range(HEAD_PAIRS):
        state_s[p] = states[p]

    for n in range(RET_V // MXU_N):
        cs = slice(n * MXU_N, (n + 1) * MXU_N)
        g = _dot(h_s[...], wi_ref[0, :, O_GSW + n * MXU_N:O_GSW + (n + 1) * MXU_N])
        gated = _with_token(g * _sigmoid(g) * ret_s[:, cs], next_tile_parts(n, n + 1))
        ra_s[:, cs] = gated.astype(BF16)

    pbuf_s[POOL_HALO:POOL_HALO + rows, :] = _dot(h_s[...], wi_ref[0, :, O_POOL:O_GATES])
    head_pos = s_idx * rows + lax.broadcasted_iota(jnp.int32, (POOL_HALO, POOL_GROUP_DIM), 0)
    pooled = []
    for g, w in enumerate(POOL_WINDOWS):
        cs = slice(g * POOL_GROUP_DIM, (g + 1) * POOL_GROUP_DIM)
        ext = pbuf_s[:, cs]
        cur = ext[POOL_HALO:]
        span = 1
        while span < w:
            ext = ext + pltpu.roll(ext, span, axis=0)
            span *= 2
        acc = ext[POOL_HALO:]
        head_inv = 1.0 / jnp.minimum(head_pos + 1, w).astype(F32)
        mean = jnp.concatenate([acc[:POOL_HALO] * head_inv, acc[POOL_HALO:] * (1.0 / w)], axis=0)
        pooled.append((mean - cur).astype(BF16))
    pbuf_s[0:POOL_HALO, :] = pbuf_s[rows:rows + POOL_HALO, :]
    for gp in range(len(POOL_WINDOWS) // 2):
        cs = slice(gp * MXU_N, (gp + 1) * MXU_N)
        pair = jnp.concatenate([pooled[2 * gp], pooled[2 * gp + 1]], axis=1)
        y = _dot(pair, wpg_ref[0, gp]) * ps_ref[0, :, cs]
        pa_s[:, cs] = y.astype(BF16)

    n_out = D_MODEL // MXU_N
    for n in range(n_out):
        cs = slice(n * MXU_N, (n + 1) * MXU_N)
        a_ret = _dot(h_s[...], wi_ref[0, :, O_GATES + n * MXU_N:O_GATES + (n + 1) * MXU_N])
        a_pool = _dot(h_s[...], wi_ref[0, :, O_GATES + D_MODEL + n * MXU_N:
                                        O_GATES + D_MODEL + (n + 1) * MXU_N])
        ret_d = _dot(ra_s[...], wro_ref[0, :, cs])
        pool_d = _dot(pa_s[...], wpo_ref[0, :, cs])
        merged = _sigmoid(a_ret) * ret_d + _sigmoid(a_pool) * pool_d
        mg_s[:, cs] = _with_token(merged, next_tile_parts(4 + n, 5 + n)).astype(BF16)

    for n in range(n_out):
        cs = slice(n * MXU_N, (n + 1) * MXU_N)
        y = _with_token(_dot(mg_s[...], wout_ref[0, :, cs]), next_tile_parts(8 + n, 9 + n))
        o_ref[0, :, cs] = x_ref[0, :, cs] + gate_res[:, cs] * y


def _mixer_call(layer, x, mod, norm1, cos, sin, wts, tabs):
    bsz, seq, d = x.shape
    rows = MIX_ROWS
    wi, wro, wpg, ps, wpo, wout = wts
    dtab, qd, kd, cd = tabs

    def lw(arr):
        shape = (1,) + arr.shape[1:]
        zeros = (0,) * (arr.ndim - 1)
        return _resident(shape, lambda b, s: (layer,) + zeros)

    def const(arr):
        zeros = (0,) * arr.ndim
        return _resident(arr.shape, lambda b, s: zeros)

    n_s = seq // rows

    def next_tile(b, s):
        flat = jnp.minimum(b * n_s + s + 1, bsz * n_s - 1)
        return (flat // n_s, flat % n_s, 0)

    in_specs = [
        pl.BlockSpec((1, rows, d), lambda b, s: (b, s, 0)),
        pl.BlockSpec((1, rows, d), next_tile),
        const(mod),
        lw(norm1),
        const(cos), const(sin),
        lw(wi), lw(wro), lw(wpg), lw(ps), lw(wpo), lw(wout),
        const(dtab), const(qd), const(kd), const(cd),
    ]
    scratch = [
        pltpu.VMEM((rows, d), BF16),
        pltpu.VMEM((rows, d), BF16),
        pltpu.VMEM((rows, RET_QK), F32),
        pltpu.VMEM((rows, RET_QK), F32),
        pltpu.VMEM((rows, RET_V), BF16),
        pltpu.VMEM((rows, RET_V), F32),
        pltpu.VMEM((rows, RET_V), BF16),
        pltpu.VMEM((rows + POOL_HALO, POOL_DIM), F32),
        pltpu.VMEM((rows, POOL_DIM), BF16),
        pltpu.VMEM((rows, d), BF16),
        pltpu.VMEM((HEAD_PAIRS, LANES, 2 * RET_V_DIM), F32),
    ]
    return pl.pallas_call(
        _mixer_kernel,
        grid=(bsz, seq // rows),
        in_specs=in_specs,
        out_specs=pl.BlockSpec((1, rows, d), lambda b, s: (b, s, 0)),
        out_shape=jax.ShapeDtypeStruct(x.shape, F32),
        scratch_shapes=scratch,
        compiler_params=pltpu.CompilerParams(
            dimension_semantics=("arbitrary", "arbitrary"),
            vmem_limit_bytes=VMEM_LIMIT),
        name=f"mixer_l{layer}",
    )(x, x, mod, norm1, cos, sin, wi, wro, wpg, ps, wpo, wout,
      dtab, qd, kd, cd)


def _ffn_kernel(x_ref, xn_ref, mod_ref, n2_ref, wi_ref, wo_ref, fn_ref, o_ref,
                h_s, hn_s, act_s, *, final, tiles_per_seq):
    n_tiles = pl.num_programs(0)
    b_idx = pl.program_id(0) // tiles_per_seq
    bn_idx = jnp.minimum(pl.program_id(0) + 1, n_tiles - 1) // tiles_per_seq
    gate_res = mod_ref[b_idx, 5:6, :]

    def norm_mod(xv, b):
        return ((_rms_scale(xv) * n2_ref[0]) * (1.0 + mod_ref[b, 4:5, :]) + mod_ref[b, 3:4, :]).astype(BF16)

    @pl.when(pl.program_id(0) == 0)
    def _():
        hn_s[...] = norm_mod(x_ref[...], b_idx)

    n_chunks = D_FF // MXU_N
    n_parts = 8
    part = x_ref.shape[0] // n_parts
    for j in range(n_chunks):
        cs = slice(j * MXU_N, (j + 1) * MXU_N)
        cs2 = slice(D_FF + j * MXU_N, D_FF + (j + 1) * MXU_N)
        src = hn_s if j == 0 else h_s
        gate = _dot(src[...], wi_ref[0, :, cs])
        up = _dot(src[...], wi_ref[0, :, cs2])
        if j == 0:
            h_s[...] = hn_s[...]
        act = gate * _sigmoid(gate) * up
        if 3 <= j < 3 + n_parts:
            rs = slice((j - 3) * part, (j - 2) * part)
            hb = norm_mod(xn_ref[rs, :], bn_idx)
            hn_s[rs, :] = hb
            act = _with_token(act, _zero_token(hb))
        act_s[:, cs] = act.astype(BF16)

    for n in range(D_MODEL // MXU_N):
        cs = slice(n * MXU_N, (n + 1) * MXU_N)
        y = _dot(act_s[...], wo_ref[0, :, cs])
        o_ref[:, cs] = x_ref[:, cs] + gate_res[:, cs] * y

    if final:
        o_ref[...] = _rms_scale(o_ref[...]) * fn_ref[...]


def _ffn_call(layer, x2d, mod, norm2, wi, wo, final_norm, seq, final):
    n_rows, d = x2d.shape
    rows = FFN_ROWS
    tiles_per_seq = seq // rows
    last = n_rows // rows - 1

    return pl.pallas_call(
        functools.partial(_ffn_kernel, final=final, tiles_per_seq=tiles_per_seq),
        grid=(n_rows // rows,),
        in_specs=[
            pl.BlockSpec((rows, d), lambda i: (i, 0)),
            pl.BlockSpec((rows, d), lambda i: (jnp.minimum(i + 1, last), 0)),
            _resident(mod.shape, lambda i: (0, 0, 0)),
            _resident((1, 1, d), lambda i: (layer, 0, 0)),
            _resident((1, d, 2 * D_FF), lambda i: (layer, 0, 0)),
            _resident((1, D_FF, d), lambda i: (layer, 0, 0)),
            _resident((1, d), lambda i: (0, 0)),
        ],
        out_specs=pl.BlockSpec((rows, d), lambda i: (i, 0)),
        out_shape=jax.ShapeDtypeStruct(x2d.shape, F32),
        scratch_shapes=[
            pltpu.VMEM((rows, d), BF16),
            pltpu.VMEM((rows, d), BF16),
            pltpu.VMEM((rows, D_FF), BF16),
        ],
        compiler_params=pltpu.CompilerParams(
            dimension_semantics=("arbitrary",),
            vmem_limit_bytes=VMEM_LIMIT),
        name=f"swiglu_l{layer}",
    )(x2d, x2d, mod, norm2, wi, wo, final_norm)


def _retention_tables():
    heads = jnp.arange(RET_HEADS, dtype=F32)
    log_g = jnp.log1p(-(2.0 ** (-5.0 - heads)))
    idx = jnp.arange(CHUNK, dtype=F32)
    diff = idx[:, None] - idx[None, :]
    qscale = RET_QK_DIM ** -0.5
    intra = jnp.where(diff >= 0, jnp.exp(log_g[:, None, None] * jnp.maximum(diff, 0.0)), 0.0)
    dtab = intra * qscale
    q_dec = jnp.exp(log_g[:, None] * (idx + 1.0)) * qscale
    k_dec = jnp.exp(log_g[:, None] * (CHUNK - 1.0 - idx))
    chunk_dec = jnp.exp(log_g * CHUNK)
    parity = jnp.arange(LANES) // RET_QK_DIM
    pair_head = 2 * jnp.arange(HEAD_PAIRS)[:, None] + parity[None, :]
    dtab_p = jnp.concatenate([dtab[0::2], dtab[1::2]], axis=2)
    qd_p = jnp.transpose(q_dec[pair_head], (0, 2, 1))
    kd_p = jnp.transpose(k_dec[pair_head], (0, 2, 1))
    cd_p = jnp.concatenate(
        [jnp.broadcast_to(chunk_dec[0::2, None, None], (HEAD_PAIRS, 1, RET_V_DIM)),
         jnp.broadcast_to(chunk_dec[1::2, None, None], (HEAD_PAIRS, 1, RET_V_DIM))], axis=2)
    return dtab_p, qd_p, kd_p, cd_p


def _rotary_tables(seq):
    inv_freq = ROPE_BASE ** (-jnp.arange(HALF, dtype=F32) / HALF)
    ang = jnp.arange(seq, dtype=F32)[:, None] * inv_freq[None, :]
    cos = jnp.cos(ang)
    sin = jnp.sin(ang)
    cos4 = jnp.concatenate([cos, cos, cos, cos], axis=-1)
    sin4 = jnp.concatenate([-sin, sin, -sin, sin], axis=-1)
    return cos4, sin4


def kernel(x, c, w_ada, b_ada, norm1, w_in, w_ret_o, w_pool_grp, pool_scale, w_pool_o,
           w_out, norm2, w_ffn_in, w_ffn_out, final_norm):
    bsz, seq, d = x.shape
    depth = w_in.shape[0]
    assert seq % MIX_ROWS == 0 and seq % FFN_ROWS == 0 and MIX_ROWS % CHUNK == 0

    mod = _ada_call(c, w_ada, b_ada).reshape(depth, bsz, N_MOD, d)

    assert w_in.shape[2] == IN_COLS
    n_gp = len(POOL_WINDOWS) // 2
    wpg = w_pool_grp.astype(BF16).reshape(depth, n_gp, 2, POOL_GROUP_DIM, POOL_GROUP_DIM)
    wpg = jnp.einsum('lpkio,kq->lpkiqo', wpg, jnp.eye(2, dtype=BF16))
    wpg = wpg.reshape(depth, n_gp, 2 * POOL_GROUP_DIM, 2 * POOL_GROUP_DIM)
    wts = (w_in.astype(BF16), w_ret_o.astype(BF16), wpg,
           pool_scale.reshape(depth, 1, POOL_DIM), w_pool_o.astype(BF16), w_out.astype(BF16))
    wi = w_ffn_in.astype(BF16)
    wo = w_ffn_out.astype(BF16)
    n1 = norm1.reshape(depth, 1, d)
    n2 = norm2.reshape(depth, 1, d)
    fn = final_norm.reshape(1, d)

    tabs = _retention_tables()
    cos4, sin4 = _rotary_tables(seq)

    for l in range(depth):
        x = _mixer_call(l, x, mod[l], n1, cos4, sin4, wts, tabs)
        x2d = _ffn_call(l, x.reshape(bsz * seq, d), mod[l], n2, wi, wo, fn, seq,
                        final=(l == depth - 1))
        x = x2d.reshape(bsz, seq, d)
    return x
```

```python
import functools

import jax
import jax.numpy as jnp
from jax import lax
from jax.experimental import pallas as pl
from jax.experimental.pallas import tpu as pltpu

D_MODEL = 1024
RET_HEADS = 8
RET_QK_DIM = 64
RET_V_DIM = 128
RET_QK = RET_HEADS * RET_QK_DIM
RET_V = RET_HEADS * RET_V_DIM
CHUNK = 128
ROPE_BASE = 10000.0
POOL_WINDOWS = (2, 4, 8, 16)
POOL_DIM = 512
POOL_GROUP_DIM = 128
POOL_HALO = 16
D_FF = 2816
N_MOD = 6
EPS = 1e-6

LANES = 128
SUBLANES = 8
MXU_N = 256
HALF = RET_QK_DIM // 2
HEAD_PAIRS = RET_HEADS // 2
O_QK = 0
O_V = 2 * RET_QK
O_GSW = O_V + RET_V
O_POOL = O_GSW + RET_V
O_GATES = O_POOL + POOL_DIM
IN_COLS = O_GATES + 2 * D_MODEL

MIX_ROWS = 512
TOKEN_PHASES = ("gate", "merged", "out", "value")
CHUNKS_PER_PHASE = D_MODEL // MXU_N
NEXT_TILE_PARTS = len(TOKEN_PHASES) * CHUNKS_PER_PHASE
FFN_ROWS = 512
FFN_NEXT_TILE_PARTS = 8
V7X_VMEM_BYTES = 64 * 1024 * 1024
VMEM_LIMIT = V7X_VMEM_BYTES * 7 // 8

F32 = jnp.float32
BF16 = jnp.bfloat16


def _dot(a, b):
    return jnp.dot(a, b, preferred_element_type=F32)


def _sigmoid(x):
    return 1.0 / (1.0 + jnp.exp(-x))


def _rms_scale(x):
    return x * lax.rsqrt(jnp.mean(x * x, axis=-1, keepdims=True) + EPS)


def _zero_token(v):
    bits = pltpu.bitcast(v, jnp.int32)
    n_rows, n_cols = bits.shape
    acc = bits[0:SUBLANES]
    for i in range(1, n_rows // SUBLANES):
        acc = acc | bits[i * SUBLANES:(i + 1) * SUBLANES]
    tok = acc[:, 0:LANES]
    for j in range(1, n_cols // LANES):
        tok = tok | acc[:, j * LANES:(j + 1) * LANES]
    tok = lax.shift_right_logical(lax.shift_right_logical(tok, 16), 16)
    return tok.astype(F32)


def _with_token(val, tok):
    head = val[0:SUBLANES] + jnp.concatenate([tok] * (val.shape[1] // LANES), axis=1)
    return jnp.concatenate([head, val[SUBLANES:]], axis=0)


def _resident(shape, index_map):
    return pl.BlockSpec(shape, index_map, pipeline_mode=pl.Buffered(1))


def _ada_kernel(c_ref, w_ref, b_ref, o_ref):
    c = c_ref[...]
    ca = (c * _sigmoid(c)).astype(BF16)
    o_ref[0] = _dot(ca, w_ref[0].astype(BF16)) + b_ref[0]


def _ada_call(c, w_ada, b_ada):
    depth, d, n = w_ada.shape
    bsz = c.shape[0]
    bn = 1024
    return pl.pallas_call(
        _ada_kernel,
        grid=(depth, n // bn),
        in_specs=[
            pl.BlockSpec((bsz, d), lambda l, j: (0, 0)),
            pl.BlockSpec((1, d, bn), lambda l, j: (l, 0, j)),
            pl.BlockSpec((1, 1, bn), lambda l, j: (l, 0, j)),
        ],
        out_specs=pl.BlockSpec((1, bsz, bn), lambda l, j: (l, 0, j)),
        out_shape=jax.ShapeDtypeStruct((depth, bsz, n), F32),
        compiler_params=pltpu.CompilerParams(
            dimension_semantics=("arbitrary", "arbitrary"),
            vmem_limit_bytes=VMEM_LIMIT),
        name="adaln_mod",
    )(c, w_ada, b_ada.reshape(depth, 1, n))


def _mixer_kernel(x_ref, xn_ref, mod_ref, n1_ref, cos_ref, sin_ref,
                  wi_ref, wro_ref, wpg_ref, ps_ref, wpo_ref, wout_ref,
                  dtab_ref, qd_ref, kd_ref, cd_ref,
                  o_ref,
                  h_s, hn_s, q_s, k_s, v_s, ret_s, ra_s, pbuf_s, pa_s, mg_s, state_s):
    rows = x_ref.shape[1]
    s_idx = pl.program_id(1)
    b_idx = pl.program_id(0)
    n_s = pl.num_programs(1)
    bn_idx = jnp.minimum(b_idx * n_s + s_idx + 1, pl.num_programs(0) * n_s - 1) // n_s
    gate_res = mod_ref[b_idx, 2:3, :]

    def norm_mod(xv, b):
        return ((_rms_scale(xv) * n1_ref[0]) * (1.0 + mod_ref[b, 1:2, :]) + mod_ref[b, 0:1, :]).astype(BF16)

    @pl.when(s_idx == 0)
    def _():
        state_s[...] = jnp.zeros_like(state_s)
        pbuf_s[0:POOL_HALO, :] = jnp.zeros((POOL_HALO, POOL_DIM), F32)

    @pl.when((s_idx == 0) & (b_idx == 0))
    def _():
        hn_s[...] = norm_mod(x_ref[0], b_idx)

    part = rows // NEXT_TILE_PARTS

    def next_tile_part(phase, n):
        i = TOKEN_PHASES.index(phase) * CHUNKS_PER_PHASE + n
        rs = slice(i * part, (i + 1) * part)
        hb = norm_mod(xn_ref[0, rs, :], bn_idx)
        hn_s[rs, :] = hb
        return _zero_token(hb)

    row0 = pl.multiple_of(s_idx * rows, rows)
    cos = cos_ref[pl.ds(row0, rows), :]
    sin = sin_ref[pl.ds(row0, rows), :]
    x1_lane = (lax.broadcasted_iota(jnp.int32, (rows, LANES), 1) & HALF) == 0

    def rotary(t):
        partner = jnp.where(x1_lane, pltpu.roll(t, LANES - HALF, axis=1), pltpu.roll(t, HALF, axis=1))
        return t * cos + partner * sin

    for n in range(2 * RET_QK // MXU_N):
        cs = slice(O_QK + n * MXU_N, O_QK + (n + 1) * MXU_N)
        qk = _dot(hn_s[...], wi_ref[0, :, cs])
        dst = q_s if n < RET_QK // MXU_N else k_s
        for j in range(MXU_N // LANES):
            col = (n % (RET_QK // MXU_N)) * MXU_N + j * LANES
            dst[:, col:col + LANES] = rotary(qk[:, j * LANES:(j + 1) * LANES])
    h_s[...] = hn_s[...]

    for n in range(RET_V // MXU_N):
        cs = slice(n * MXU_N, (n + 1) * MXU_N)
        cw = slice(O_V + n * MXU_N, O_V + (n + 1) * MXU_N)
        v = _with_token(_dot(h_s[...], wi_ref[0, :, cw]), next_tile_part("value", n))
        v_s[:, cs] = v.astype(BF16)

    lane = lax.broadcasted_iota(jnp.int32, (CHUNK, LANES), 1)
    even_lane = lane < RET_QK_DIM
    row_odd = lax.broadcasted_iota(jnp.int32, (LANES, 2 * RET_V_DIM), 0) >= RET_QK_DIM
    col_odd = lax.broadcasted_iota(jnp.int32, (LANES, 2 * RET_V_DIM), 1) >= RET_V_DIM
    same_head = row_odd == col_odd
    zero_v = jnp.zeros((CHUNK, RET_V_DIM), BF16)
    states = [state_s[p] for p in range(HEAD_PAIRS)]
    for c in range(rows // CHUNK):
        rs = slice(c * CHUNK, (c + 1) * CHUNK)
        for p in range(HEAD_PAIRS):
            sl = slice(p * LANES, (p + 1) * LANES)
            vs = slice(p * 2 * RET_V_DIM, (p + 1) * 2 * RET_V_DIM)
            st = states[p]
            qc = q_s[rs, sl]
            kc = k_s[rs, sl]
            v_pair = v_s[rs, vs]
            k_cat = jnp.concatenate([jnp.where(even_lane, kc, 0.0).astype(BF16),
                                     jnp.where(even_lane, 0.0, kc).astype(BF16)], axis=0)
            scores = lax.dot_general(qc.astype(BF16), k_cat, (((1,), (1,)), ((), ())),
                                     preferred_element_type=F32)
            sb = (scores * dtab_ref[p]).astype(BF16)
            v_diag = jnp.concatenate(
                [jnp.concatenate([v_pair[:, :RET_V_DIM], zero_v], axis=1),
                 jnp.concatenate([zero_v, v_pair[:, RET_V_DIM:]], axis=1)], axis=0)
            qd = (qc * qd_ref[p]).astype(BF16)
            out = _dot(sb, v_diag) + _dot(qd, st.astype(BF16))
            kd = (kc * kd_ref[p]).astype(BF16)
            kv = lax.dot_general(kd, v_pair, (((0,), (0,)), ((), ())),
                                 preferred_element_type=F32)
            states[p] = st * cd_ref[p] + jnp.where(same_head, kv, 0.0)
            for hh in range(2):
                hs = slice((2 * p + hh) * RET_V_DIM, (2 * p + hh + 1) * RET_V_DIM)
                ret_s[rs, hs] = _rms_scale(out[:, hh * RET_V_DIM:(hh + 1) * RET_V_DIM])
    for p in range(HEAD_PAIRS):
        state_s[p] = states[p]

    pbuf_s[POOL_HALO:POOL_HALO + rows, :] = _dot(h_s[...], wi_ref[0, :, O_POOL:O_GATES])
    head_pos = s_idx * rows + lax.broadcasted_iota(jnp.int32, (POOL_HALO, POOL_GROUP_DIM), 0)
    pooled = []
    for n in range(RET_V // MXU_N):
        cs = slice(n * MXU_N, (n + 1) * MXU_N)
        g = _dot(h_s[...], wi_ref[0, :, O_GSW + n * MXU_N:O_GSW + (n + 1) * MXU_N])
        gated = _with_token(g * _sigmoid(g) * ret_s[:, cs], next_tile_part("gate", n))
        ra_s[:, cs] = gated.astype(BF16)

        w = POOL_WINDOWS[n]
        cs = slice(n * POOL_GROUP_DIM, (n + 1) * POOL_GROUP_DIM)
        ext = pbuf_s[:, cs]
        cur = ext[POOL_HALO:]
        span = 1
        while span < w:
            ext = ext + pltpu.roll(ext, span, axis=0)
            span *= 2
        acc = ext[POOL_HALO:]
        head_inv = 1.0 / jnp.minimum(head_pos + 1, w).astype(F32)
        mean = jnp.concatenate([acc[:POOL_HALO] * head_inv, acc[POOL_HALO:] * (1.0 / w)], axis=0)
        pooled.append((mean - cur).astype(BF16))
    pbuf_s[0:POOL_HALO, :] = pbuf_s[rows:rows + POOL_HALO, :]
    for gp in range(len(POOL_WINDOWS) // 2):
        cs = slice(gp * MXU_N, (gp + 1) * MXU_N)
        pair = jnp.concatenate([pooled[2 * gp], pooled[2 * gp + 1]], axis=1)
        y = _dot(pair, wpg_ref[0, gp]) * ps_ref[0, :, cs]
        pa_s[:, cs] = y.astype(BF16)

    n_out = D_MODEL // MXU_N
    for n in range(n_out):
        cs = slice(n * MXU_N, (n + 1) * MXU_N)
        a_ret = _dot(h_s[...], wi_ref[0, :, O_GATES + n * MXU_N:O_GATES + (n + 1) * MXU_N])
        a_pool = _dot(h_s[...], wi_ref[0, :, O_GATES + D_MODEL + n * MXU_N:
                                        O_GATES + D_MODEL + (n + 1) * MXU_N])
        ret_d = _dot(ra_s[...], wro_ref[0, :, cs])
        pool_d = _dot(pa_s[...], wpo_ref[0, :, cs])
        merged = _sigmoid(a_ret) * ret_d + _sigmoid(a_pool) * pool_d
        mg_s[:, cs] = _with_token(merged, next_tile_part("merged", n)).astype(BF16)

    for n in range(n_out):
        cs = slice(n * MXU_N, (n + 1) * MXU_N)
        y = _with_token(_dot(mg_s[...], wout_ref[0, :, cs]), next_tile_part("out", n))
        o_ref[0, :, cs] = x_ref[0, :, cs] + gate_res[:, cs] * y


def _mixer_call(layer, x, mod, norm1, cos, sin, wts, tabs):
    bsz, seq, d = x.shape
    rows = MIX_ROWS
    wi, wro, wpg, ps, wpo, wout = wts
    dtab, qd, kd, cd = tabs

    def lw(arr):
        shape = (1,) + arr.shape[1:]
        zeros = (0,) * (arr.ndim - 1)
        return _resident(shape, lambda b, s: (layer,) + zeros)

    def const(arr):
        zeros = (0,) * arr.ndim
        return _resident(arr.shape, lambda b, s: zeros)

    n_s = seq // rows

    def next_tile(b, s):
        flat = jnp.minimum(b * n_s + s + 1, bsz * n_s - 1)
        return (flat // n_s, flat % n_s, 0)

    in_specs = [
        pl.BlockSpec((1, rows, d), lambda b, s: (b, s, 0)),
        pl.BlockSpec((1, rows, d), next_tile),
        const(mod),
        lw(norm1),
        const(cos), const(sin),
        lw(wi), lw(wro), lw(wpg), lw(ps), lw(wpo), lw(wout),
        const(dtab), const(qd), const(kd), const(cd),
    ]
    scratch = [
        pltpu.VMEM((rows, d), BF16),
        pltpu.VMEM((rows, d), BF16),
        pltpu.VMEM((rows, RET_QK), F32),
        pltpu.VMEM((rows, RET_QK), F32),
        pltpu.VMEM((rows, RET_V), BF16),
        pltpu.VMEM((rows, RET_V), F32),
        pltpu.VMEM((rows, RET_V), BF16),
        pltpu.VMEM((rows + POOL_HALO, POOL_DIM), F32),
        pltpu.VMEM((rows, POOL_DIM), BF16),
        pltpu.VMEM((rows, d), BF16),
        pltpu.VMEM((HEAD_PAIRS, LANES, 2 * RET_V_DIM), F32),
    ]
    return pl.pallas_call(
        _mixer_kernel,
        grid=(bsz, seq // rows),
        in_specs=in_specs,
        out_specs=pl.BlockSpec((1, rows, d), lambda b, s: (b, s, 0)),
        out_shape=jax.ShapeDtypeStruct(x.shape, F32),
        scratch_shapes=scratch,
        compiler_params=pltpu.CompilerParams(
            dimension_semantics=("arbitrary", "arbitrary"),
            vmem_limit_bytes=VMEM_LIMIT),
        name=f"mixer_l{layer}",
    )(x, x, mod, norm1, cos, sin, wi, wro, wpg, ps, wpo, wout,
      dtab, qd, kd, cd)


def _ffn_kernel(x_ref, xn_ref, mod_ref, n2_ref, wi_ref, wo_ref, fn_ref, o_ref,
                h_s, hn_s, act_s, *, final, tiles_per_seq):
    n_tiles = pl.num_programs(0)
    b_idx = pl.program_id(0) // tiles_per_seq
    bn_idx = jnp.minimum(pl.program_id(0) + 1, n_tiles - 1) // tiles_per_seq
    gate_res = mod_ref[b_idx, 5:6, :]

    def norm_mod(xv, b):
        return ((_rms_scale(xv) * n2_ref[0]) * (1.0 + mod_ref[b, 4:5, :]) + mod_ref[b, 3:4, :]).astype(BF16)

    @pl.when(pl.program_id(0) == 0)
    def _():
        hn_s[...] = norm_mod(x_ref[...], b_idx)

    n_chunks = D_FF // MXU_N
    part = x_ref.shape[0] // FFN_NEXT_TILE_PARTS
    first = n_chunks - FFN_NEXT_TILE_PARTS
    for j in range(n_chunks):
        cs = slice(j * MXU_N, (j + 1) * MXU_N)
        cs2 = slice(D_FF + j * MXU_N, D_FF + (j + 1) * MXU_N)
        src = hn_s if j == 0 else h_s
        gate = _dot(src[...], wi_ref[0, :, cs])
        up = _dot(src[...], wi_ref[0, :, cs2])
        if j == 0:
            h_s[...] = hn_s[...]
        act = gate * _sigmoid(gate) * up
        if j >= first:
            rs = slice((j - first) * part, (j - first + 1) * part)
            hb = norm_mod(xn_ref[rs, :], bn_idx)
            hn_s[rs, :] = hb
            act = _with_token(act, _zero_token(hb))
        act_s[:, cs] = act.astype(BF16)

    for n in range(D_MODEL // MXU_N):
        cs = slice(n * MXU_N, (n + 1) * MXU_N)
        y = _dot(act_s[...], wo_ref[0, :, cs])
        o_ref[:, cs] = x_ref[:, cs] + gate_res[:, cs] * y

    if final:
        o_ref[...] = _rms_scale(o_ref[...]) * fn_ref[...]


def _ffn_call(layer, x2d, mod, norm2, wi, wo, final_norm, seq, final):
    n_rows, d = x2d.shape
    rows = FFN_ROWS
    tiles_per_seq = seq // rows
    last = n_rows // rows - 1

    return pl.pallas_call(
        functools.partial(_ffn_kernel, final=final, tiles_per_seq=tiles_per_seq),
        grid=(n_rows // rows,),
        in_specs=[
            pl.BlockSpec((rows, d), lambda i: (i, 0)),
            pl.BlockSpec((rows, d), lambda i: (jnp.minimum(i + 1, last), 0)),
            _resident(mod.shape, lambda i: (0, 0, 0)),
            _resident((1, 1, d), lambda i: (layer, 0, 0)),
            _resident((1, d, 2 * D_FF), lambda i: (layer, 0, 0)),
            _resident((1, D_FF, d), lambda i: (layer, 0, 0)),
            _resident((1, d), lambda i: (0, 0)),
        ],
        out_specs=pl.BlockSpec((rows, d), lambda i: (i, 0)),
        out_shape=jax.ShapeDtypeStruct(x2d.shape, F32),
        scratch_shapes=[
            pltpu.VMEM((rows, d), BF16),
            pltpu.VMEM((rows, d), BF16),
            pltpu.VMEM((rows, D_FF), BF16),
        ],
        compiler_params=pltpu.CompilerParams(
            dimension_semantics=("arbitrary",),
            vmem_limit_bytes=VMEM_LIMIT),
        name=f"swiglu_l{layer}",
    )(x2d, x2d, mod, norm2, wi, wo, final_norm)


def _retention_tables():
    heads = jnp.arange(RET_HEADS, dtype=F32)
    log_g = jnp.log1p(-(2.0 ** (-5.0 - heads)))
    idx = jnp.arange(CHUNK, dtype=F32)
    diff = idx[:, None] - idx[None, :]
    qscale = RET_QK_DIM ** -0.5
    intra = jnp.where(diff >= 0, jnp.exp(log_g[:, None, None] * jnp.maximum(diff, 0.0)), 0.0)
    dtab = intra * qscale
    q_dec = jnp.exp(log_g[:, None] * (idx + 1.0)) * qscale
    k_dec = jnp.exp(log_g[:, None] * (CHUNK - 1.0 - idx))
    chunk_dec = jnp.exp(log_g * CHUNK)
    parity = jnp.arange(LANES) // RET_QK_DIM
    pair_head = 2 * jnp.arange(HEAD_PAIRS)[:, None] + parity[None, :]
    dtab_p = jnp.concatenate([dtab[0::2], dtab[1::2]], axis=2)
    qd_p = jnp.transpose(q_dec[pair_head], (0, 2, 1))
    kd_p = jnp.transpose(k_dec[pair_head], (0, 2, 1))
    cd_p = jnp.concatenate(
        [jnp.broadcast_to(chunk_dec[0::2, None, None], (HEAD_PAIRS, 1, RET_V_DIM)),
         jnp.broadcast_to(chunk_dec[1::2, None, None], (HEAD_PAIRS, 1, RET_V_DIM))], axis=2)
    return dtab_p, qd_p, kd_p, cd_p


def _rotary_tables(seq):
    inv_freq = ROPE_BASE ** (-jnp.arange(HALF, dtype=F32) / HALF)
    ang = jnp.arange(seq, dtype=F32)[:, None] * inv_freq[None, :]
    cos = jnp.cos(ang)
    sin = jnp.sin(ang)
    cos4 = jnp.concatenate([cos, cos, cos, cos], axis=-1)
    sin4 = jnp.concatenate([-sin, sin, -sin, sin], axis=-1)
    return cos4, sin4


def kernel(x, c, w_ada, b_ada, norm1, w_in, w_ret_o, w_pool_grp, pool_scale, w_pool_o,
           w_out, norm2, w_ffn_in, w_ffn_out, final_norm):
    bsz, seq, d = x.shape
    depth = w_in.shape[0]
    assert seq % MIX_ROWS == 0 and seq % FFN_ROWS == 0 and MIX_ROWS % CHUNK == 0

    mod = _ada_call(c, w_ada, b_ada).reshape(depth, bsz, N_MOD, d)

    assert w_in.shape[2] == IN_COLS
    n_gp = len(POOL_WINDOWS) // 2
    wpg = w_pool_grp.astype(BF16).reshape(depth, n_gp, 2, POOL_GROUP_DIM, POOL_GROUP_DIM)
    wpg = jnp.einsum('lpkio,kq->lpkiqo', wpg, jnp.eye(2, dtype=BF16))
    wpg = wpg.reshape(depth, n_gp, 2 * POOL_GROUP_DIM, 2 * POOL_GROUP_DIM)
    wts = (w_in.astype(BF16), w_ret_o.astype(BF16), wpg,
           pool_scale.reshape(depth, 1, POOL_DIM), w_pool_o.astype(BF16), w_out.astype(BF16))
    wi = w_ffn_in.astype(BF16)
    wo = w_ffn_out.astype(BF16)
    n1 = norm1.reshape(depth, 1, d)
    n2 = norm2.reshape(depth, 1, d)
    fn = final_norm.reshape(1, d)

    tabs = _retention_tables()
    cos4, sin4 = _rotary_tables(seq)

    for l in range(depth):
        x = _mixer_call(l, x, mod[l], n1, cos4, sin4, wts, tabs)
        x2d = _ffn_call(l, x.reshape(bsz * seq, d), mod[l], n2, wi, wo, fn, seq,
                        final=(l == depth - 1))
        x = x2d.reshape(bsz, seq, d)
    return x
```
